```python
import math
import jax
import jax.numpy as jnp
from jax import lax
import numpy as np

D_MODEL = 1024
BATCH = 2
SEQ = 8192
DEPTH = 4
DEC_BATCH = 128
DEC_SEQ = 8
PAST_LEN = 2048
PAGE_SIZE = 128

N_META = 16
N_MIXERS = 2
N_ATT_LAYERS = (DEPTH + N_MIXERS - 1) // N_MIXERS
N_RET_LAYERS = DEPTH // N_MIXERS
DA_HEADS = 8
DA_DK = D_MODEL // DA_HEADS // 2
DA_DV = 2 * DA_DK
ROPE_THETA = 500000.0
ROPE_DIMS = DA_DK // 4
Q_BLOCK = 128
RET_HEADS = D_MODEL // 256
RET_DK = D_MODEL // RET_HEADS
RET_DV = 2 * RET_DK
RET_QK = RET_HEADS * RET_DK
RET_VD = RET_HEADS * RET_DV
RET_CHUNK = 128
RET_THETA = 10000.0
_FFN_RAW = -(-8 * D_MODEL // 3)
FFN_HIDDEN = -(-_FFN_RAW // 256) * 256
RMS_EPS = 1e-6

kernel_name = 'hybrid_diffattn_retention_step'


def _rms(x):
    xf = x.astype(jnp.float32)
    return xf * lax.rsqrt(jnp.mean(xf * xf, axis=-1, keepdims=True) + RMS_EPS)


def _rmsnorm(x, g):
    return (_rms(x) * g.astype(jnp.float32)).astype(x.dtype)


def _rope_partial(x, pos):
    half = ROPE_DIMS // 2
    inv = ROPE_THETA ** (-jnp.arange(0, ROPE_DIMS, 2, dtype=jnp.float32) / ROPE_DIMS)
    ang = pos.astype(jnp.float32)[:, None] * inv[None, :]
    cos = jnp.cos(ang)[:, None, None, :]
    sin = jnp.sin(ang)[:, None, None, :]
    xf = x.astype(jnp.float32)
    x1 = xf[..., :half]
    x2 = xf[..., half:ROPE_DIMS]
    out = jnp.concatenate([x1 * cos - x2 * sin, x2 * cos + x1 * sin, xf[..., ROPE_DIMS:]], axis=-1)
    return out.astype(x.dtype)


def _ret_rotate(x, pos):
    angle = 1.0 / (RET_THETA ** jnp.linspace(0.0, 1.0, RET_DK // 2, dtype=jnp.float32))
    angle = jnp.repeat(angle, 2)
    ang = pos.astype(jnp.float32)[:, None] * angle[None, :]
    cos = jnp.cos(ang)[:, None, :]
    sin = jnp.sin(ang)[:, None, :]
    xf = x.astype(jnp.float32)
    rot = jnp.stack([-xf[..., 1::2], xf[..., 0::2]], axis=-1).reshape(xf.shape)
    return (xf * cos + rot * sin).astype(x.dtype)


def _da_project(h, w_in, pos):
    B, L, _ = h.shape
    qkv = h @ w_in
    q = qkv[..., :D_MODEL].reshape(B, L, DA_HEADS, 2, DA_DK)
    k = qkv[..., D_MODEL:2 * D_MODEL].reshape(B, L, DA_HEADS, 2, DA_DK)
    v = qkv[..., 2 * D_MODEL:].reshape(B, L, DA_HEADS, DA_DV)
    return _rope_partial(q, pos), _rope_partial(k, pos), v


def _da_lambda(lq1, lk1, lq2, lk2, lam_init):
    f = lambda a: a.astype(jnp.float32)
    return jnp.exp(jnp.sum(f(lq1) * f(lk1))) - jnp.exp(jnp.sum(f(lq2) * f(lk2))) + lam_init


def _da_weights(s, lam):
    p = jax.nn.softmax(s, axis=-1)
    return p[:, :, 0] - lam * p[:, :, 1]


def _da_prompt(q, k, v, lam):
    B, L = q.shape[:2]
    n_blk = -(-L // Q_BLOCK)
    pad = n_blk * Q_BLOCK - L
    padl = lambda a: jnp.pad(a, [(0, 0), (0, pad)] + [(0, 0)] * (a.ndim - 2))
    q, k, v = padl(q), padl(k), padl(v)
    kpos = jnp.arange(n_blk * Q_BLOCK)
    qb = q.reshape(B, n_blk, Q_BLOCK, DA_HEADS, 2, DA_DK).swapaxes(0, 1)
    scale = DA_DK ** -0.5

    def block(args):
        qi, i = args
        s = jnp.einsum('bqhcd,bkhcd->bhcqk', qi, k).astype(jnp.float32) * scale
        qpos = i * Q_BLOCK + jnp.arange(Q_BLOCK)
        s = jnp.where(kpos[None, :] <= qpos[:, None], s, -jnp.inf)
        a = _da_weights(s, lam)
        return jnp.einsum('bhqk,bkhe->bqhe', a.astype(v.dtype), v)

    o = lax.map(block, (qb, jnp.arange(n_blk)))
    o = o.swapaxes(0, 1).reshape(B, n_blk * Q_BLOCK, DA_HEADS, DA_DV)
    return o[:, :L]


def _da_sample(q, k_new, v_new, k_past, v_past, lam):
    P, T = k_past.shape[1], q.shape[1]
    scale = DA_DK ** -0.5
    s_past = jnp.einsum('bqhcd,bkhcd->bhcqk', q, k_past).astype(jnp.float32) * scale
    s_new = jnp.einsum('bqhcd,bkhcd->bhcqk', q, k_new).astype(jnp.float32) * scale
    causal = jnp.tril(jnp.ones((T, T), dtype=bool))
    s_new = jnp.where(causal, s_new, -jnp.inf)
    a = _da_weights(jnp.concatenate([s_past, s_new], axis=-1), lam).astype(v_new.dtype)
    return (jnp.einsum('bhqk,bkhe->bqhe', a[..., :P], v_past)
            + jnp.einsum('bhqk,bkhe->bqhe', a[..., P:], v_new))


def _da_out(o, subln_g, lam_init, w_out):
    B, L = o.shape[:2]
    o = _rmsnorm(o, subln_g) * (1.0 - lam_init)
    return o.reshape(B, L, DA_HEADS * DA_DV) @ w_out


def _ret_project(h, w_in, pos):
    B, L, _ = h.shape
    z = h @ w_in
    q = z[..., :RET_QK].reshape(B, L, RET_HEADS, RET_DK)
    k = z[..., RET_QK:2 * RET_QK].reshape(B, L, RET_HEADS, RET_DK)
    v = z[..., 2 * RET_QK:2 * RET_QK + RET_VD].reshape(B, L, RET_HEADS, RET_DV)
    g = z[..., 2 * RET_QK + RET_VD:]
    return _ret_rotate(q, pos), _ret_rotate(k, pos) * (RET_DK ** -0.5), v, g


def _ret_log_decay():
    return jnp.log(1.0 - 2.0 ** (-5.0 - jnp.arange(RET_HEADS, dtype=jnp.float32)))


def _ret_chunk(state, q, k, v, log_g):
    C = q.shape[1]
    qf, kf, vf = q.astype(jnp.float32), k.astype(jnp.float32), v.astype(jnp.float32)
    state = state.astype(jnp.float32)
    n = jnp.arange(C, dtype=jnp.float32)
    diff = n[:, None] - n[None, :]
    dmask = jnp.where(diff >= 0, jnp.exp(log_g[:, None, None] * jnp.maximum(diff, 0.0)), 0.0)
    s = jnp.einsum('bqhd,bkhd->bhqk', qf, kf) * dmask[None]
    inner = jnp.einsum('bhqk,bkhe->bqhe', s, vf)
    q_decay = jnp.exp(log_g[None, :] * (n[:, None] + 1.0))
    cross = jnp.einsum('bqhd,bhde->bqhe', qf * q_decay[None, :, :, None], state)
    k_decay = jnp.exp(log_g[None, :] * (C - 1.0 - n)[:, None])
    new_state = (jnp.exp(log_g * C)[None, :, None, None] * state
                 + jnp.einsum('bkhd,bkhe->bhde', kf * k_decay[None, :, :, None], vf))
    return new_state, inner + cross


def _ret_prompt(q, k, v, log_g):
    B, L = q.shape[:2]
    front = (-N_META) % RET_CHUNK
    n_c = (L + front) // RET_CHUNK

    def to_chunks(a):
        a = jnp.pad(a.astype(jnp.float32), [(0, 0), (front, 0), (0, 0), (0, 0)])
        return a.reshape(B, n_c, RET_CHUNK, a.shape[2], a.shape[3]).swapaxes(0, 1)

    state0 = jnp.zeros((B, RET_HEADS, RET_DK, RET_DV), jnp.float32)
    state, o = lax.scan(lambda st, xs: _ret_chunk(st, xs[0], xs[1], xs[2], log_g),
                        state0, (to_chunks(q), to_chunks(k), to_chunks(v)))
    o = o.swapaxes(0, 1).reshape(B, n_c * RET_CHUNK, RET_HEADS, RET_DV)[:, front:]
    return o, state


def _ret_out(o, g, w_out):
    B, L = o.shape[:2]
    o = _rms(o).astype(g.dtype).reshape(B, L, RET_VD)
    return (jax.nn.silu(g) * o) @ w_out


def _ffn(h, w_in, w_out):
    gu = h @ w_in
    return (jax.nn.silu(gu[..., :FFN_HIDDEN]) * gu[..., FFN_HIDDEN:]) @ w_out


def setup_inputs(seed: int = 0) -> dict:
    key = jax.random.key(seed)
    ks = jax.random.split(key, 24)
    f32 = jnp.float32
    nrm = lambda k, shape, scale: jax.random.normal(k, shape, f32) * scale
    n_pages = PAST_LEN // PAGE_SIZE
    n_used = DEC_BATCH * n_pages
    n_pool = n_used + -(-n_used // 4)
    page_table = jax.random.permutation(ks[5], n_pool)[:n_used].reshape(DEC_BATCH, n_pages).astype(jnp.int32)
    return {
        'x_prompt': nrm(ks[0], (BATCH, SEQ, D_MODEL), 1.0),
        'x_sample': nrm(ks[1], (DEC_BATCH, DEC_SEQ, D_MODEL), 1.0),
        'cache_k': nrm(ks[2], (N_ATT_LAYERS, n_pool, PAGE_SIZE, DA_HEADS, 2, DA_DK), 1.0),
        'cache_v': nrm(ks[3], (N_ATT_LAYERS, n_pool, PAGE_SIZE, DA_HEADS, DA_DV), 1.0),
        'state_ret': nrm(ks[4], (N_RET_LAYERS, DEC_BATCH, RET_HEADS, RET_DK, RET_DV), 0.1),
        'page_table': page_table,
        'meta_tokens': nrm(ks[6], (N_META, D_MODEL), 1.0),
        'norm_mix': 1.0 + nrm(ks[7], (DEPTH, D_MODEL), 0.02),
        'norm_ffn': 1.0 + nrm(ks[8], (DEPTH, D_MODEL), 0.02),
        'norm_out': 1.0 + nrm(ks[9], (D_MODEL,), 0.02),
        'da_w_in': nrm(ks[10], (N_ATT_LAYERS, D_MODEL, 3 * D_MODEL), D_MODEL ** -0.5),
        'da_w_out': nrm(ks[11], (N_ATT_LAYERS, DA_HEADS * DA_DV, D_MODEL), (DA_HEADS * DA_DV) ** -0.5),
        'da_lambda_q1': nrm(ks[12], (N_ATT_LAYERS, DA_DK), 0.1),
        'da_lambda_k1': nrm(ks[13], (N_ATT_LAYERS, DA_DK), 0.1),
        'da_lambda_q2': nrm(ks[14], (N_ATT_LAYERS, DA_DK), 0.1),
        'da_lambda_k2': nrm(ks[15], (N_ATT_LAYERS, DA_DK), 0.1),
        'da_subln': 1.0 + nrm(ks[16], (N_ATT_LAYERS, DA_DV), 0.02),
        'ret_w_in': nrm(ks[17], (N_RET_LAYERS, D_MODEL, 2 * RET_QK + 2 * RET_VD), D_MODEL ** -0.5),
        'ret_w_out': nrm(ks[18], (N_RET_LAYERS, RET_VD, D_MODEL), RET_VD ** -0.5),
        'ffn_w_in': nrm(ks[19], (DEPTH, D_MODEL, 2 * FFN_HIDDEN), D_MODEL ** -0.5),
        'ffn_w_out': nrm(ks[20], (DEPTH, FFN_HIDDEN, D_MODEL), FFN_HIDDEN ** -0.5),
    }


def reference(x_prompt, x_sample, cache_k, cache_v, state_ret, page_table, meta_tokens,
              norm_mix, norm_ffn, norm_out, da_w_in, da_w_out, da_lambda_q1, da_lambda_k1,
              da_lambda_q2, da_lambda_k2, da_subln, ret_w_in, ret_w_out, ffn_w_in, ffn_w_out):
    B = x_prompt.shape[0]
    DB, T = x_sample.shape[:2]
    meta = jnp.broadcast_to(meta_tokens.astype(x_prompt.dtype)[None], (B, N_META, D_MODEL))
    hp = jnp.concatenate([meta, x_prompt], axis=1)
    hs = x_sample
    past = page_table.shape[1] * cache_k.shape[2]
    pos_p = jnp.arange(hp.shape[1])
    pos_s = past + jnp.arange(T)
    log_g = _ret_log_decay()
    kp_rows, vp_rows, ks_rows, vs_rows, st_p_list, st_s_list = [], [], [], [], [], []
    for i in range(DEPTH):
        j = i // N_MIXERS
        ap = _rmsnorm(hp, norm_mix[i])
        asm = _rmsnorm(hs, norm_mix[i])
        if i % N_MIXERS == 0:
            lam_init = 0.8 - 0.6 * math.exp(-0.3 * i)
            lam = _da_lambda(da_lambda_q1[j], da_lambda_k1[j], da_lambda_q2[j], da_lambda_k2[j], lam_init)
            qp, kp, vp = _da_project(ap, da_w_in[j], pos_p)
            qs, ks, vs = _da_project(asm, da_w_in[j], pos_s)
            k_past = cache_k[j, page_table].reshape(DB, past, DA_HEADS, 2, DA_DK)
            v_past = cache_v[j, page_table].reshape(DB, past, DA_HEADS, DA_DV)
            op = _da_prompt(qp, kp, vp, lam)
            osm = _da_sample(qs, ks, vs, k_past, v_past, lam)
            hp = hp + _da_out(op, da_subln[j], lam_init, da_w_out[j])
            hs = hs + _da_out(osm, da_subln[j], lam_init, da_w_out[j])
            kp_rows.append(kp)
            vp_rows.append(vp)
            ks_rows.append(ks)
            vs_rows.append(vs)
        else:
            qp, kp, vp, gp = _ret_project(ap, ret_w_in[j], pos_p)
            qs, ks, vs, gs = _ret_project(asm, ret_w_in[j], pos_s)
            op, st_p = _ret_prompt(qp, kp, vp, log_g)
            st_s, osm = _ret_chunk(state_ret[j], qs, ks, vs, log_g)
            hp = hp + _ret_out(op, gp, ret_w_out[j])
            hs = hs + _ret_out(osm, gs, ret_w_out[j])
            st_p_list.append(st_p)
            st_s_list.append(st_s)
        hp = hp + _ffn(_rmsnorm(hp, norm_ffn[i]), ffn_w_in[i], ffn_w_out[i])
        hs = hs + _ffn(_rmsnorm(hs, norm_ffn[i]), ffn_w_in[i], ffn_w_out[i])
    y_prompt = _rmsnorm(hp, norm_out)[:, N_META:]
    y_sample = _rmsnorm(hs, norm_out)
    return (y_prompt, y_sample, jnp.stack(kp_rows), jnp.stack(vp_rows), jnp.stack(ks_rows),
            jnp.stack(vs_rows), jnp.stack(st_p_list), jnp.stack(st_s_list))
```

```python
import functools
import math

import jax
import jax.numpy as jnp
from jax import lax
from jax.experimental import pallas as pl
from jax.experimental.pallas import tpu as pltpu

F32, BF16 = jnp.float32, jnp.bfloat16

D_MODEL = 1024
N_META = 16
N_MIXERS = 2
DA_HEADS = 8
DA_DK = 64
DA_DV = 128
ROPE_THETA = 500000.0
ROPE_DIMS = 16
RET_HEADS = 4
RET_DK = 256
RET_DV = 512
RET_QK = RET_HEADS * RET_DK
RET_VD = RET_HEADS * RET_DV
RET_CHUNK = 128
RET_THETA = 10000.0
FFN_HIDDEN = 2816
FFN_CHUNK = 256
RMS_EPS = 1e-6
MASK_VALUE = -1e30
LANES = 128
VMEM_LIMIT = 56 * 1024 * 1024

_NT = (((1,), (1,)), ((), ()))
_TN = (((0,), (0,)), ((), ()))


def _tiles(n_rows):
    if n_rows >= 4096:
        return dict(frame=768, tm_proj=384, tm_ffn=768, tq=768, tk=256)
    if n_rows >= 512:
        return dict(frame=512, tm_proj=512, tm_ffn=512, tq=512, tk=256)
    return dict(frame=256, tm_proj=128, tm_ffn=128, tq=256, tk=128)


def _round_up(x, m):
    return -(-x // m) * m


def _rms(x):
    return x * lax.rsqrt(jnp.mean(x * x, axis=-1, keepdims=True) + RMS_EPS)


def _resident(shape):
    return pl.BlockSpec(shape, lambda *_: (0,) * len(shape), pipeline_mode=pl.Buffered(1))


def _params(*sem):
    return pltpu.CompilerParams(dimension_semantics=sem, vmem_limit_bytes=VMEM_LIMIT)


def _da_inproj_kernel(x_ref, g_ref, wq_ref, wkt_ref, wv_ref, c_ref, s1_ref, s2_ref, ct_ref, st_ref,
                      qd_ref, ktf_ref, ktb_ref, vf_ref, vb_ref):
    hn = (_rms(x_ref[...]) * g_ref[...]).astype(BF16)
    lane = lax.broadcasted_iota(jnp.int32, (1, LANES), 1)
    comp0 = (lane < DA_DK).astype(F32)
    comp1 = 1.0 - comp0
    q = jnp.dot(hn, wq_ref[...], preferred_element_type=F32)
    c, s1, s2 = c_ref[...], s1_ref[...], s2_ref[...]
    for h in range(DA_HEADS):
        sl = slice(h * LANES, (h + 1) * LANES)
        qh = q[:, sl]
        qr = qh * c + pltpu.roll(qh, ROPE_DIMS // 2, 1) * s1 + pltpu.roll(qh, LANES - ROPE_DIMS // 2, 1) * s2
        qd_ref[0, :, sl] = (qr * comp0).astype(BF16)
        qd_ref[1, :, sl] = (qr * comp1).astype(BF16)
    kt = lax.dot_general(wkt_ref[...], hn, _NT, preferred_element_type=F32)
    ct, st = ct_ref[...], st_ref[...]
    half = ROPE_DIMS // 2
    for grp in range(2 * DA_HEADS):
        b0 = grp * DA_DK
        x1 = kt[b0:b0 + half]
        x2 = kt[b0 + half:b0 + 2 * half]
        rest = kt[b0 + 2 * half:b0 + DA_DK]
        o1 = x1 * ct - x2 * st
        o2 = x2 * ct + x1 * st
        for lo, val in ((b0, o1), (b0 + half, o2), (b0 + 2 * half, rest)):
            ktf_ref[lo:lo + val.shape[0], :] = val
            ktb_ref[lo:lo + val.shape[0], :] = val.astype(BF16)
    v = jnp.dot(hn, wv_ref[...], preferred_element_type=F32)
    vb_ref[...] = v.astype(BF16)
    for h in range(DA_HEADS):
        vf_ref[:, h, :] = v[:, h * DA_DV:(h + 1) * DA_DV]


def _da_inproj(x, g, wq, wkt, wv, tabs, n_valid, tm):
    B, F, _ = x.shape
    c, s1, s2, ct, st = tabs
    tok = lambda b, i: (b, i, 0)
    return pl.pallas_call(
        _da_inproj_kernel,
        grid=(B, F // tm),
        in_specs=[
            pl.BlockSpec((None, tm, D_MODEL), tok),
            _resident((1, D_MODEL)),
            _resident((D_MODEL, D_MODEL)),
            _resident((D_MODEL, D_MODEL)),
            _resident((D_MODEL, D_MODEL)),
            pl.BlockSpec((tm, LANES), lambda b, i: (i, 0)),
            pl.BlockSpec((tm, LANES), lambda b, i: (i, 0)),
            pl.BlockSpec((tm, LANES), lambda b, i: (i, 0)),
            pl.BlockSpec((ROPE_DIMS // 2, tm), lambda b, i: (0, i)),
            pl.BlockSpec((ROPE_DIMS // 2, tm), lambda b, i: (0, i)),
        ],
        out_specs=[
            pl.BlockSpec((2, None, tm, D_MODEL), lambda b, i: (0, b, i, 0)),
            pl.BlockSpec((None, D_MODEL, tm), lambda b, i: (b, 0, i)),
            pl.BlockSpec((None, D_MODEL, tm), lambda b, i: (b, 0, i)),
            pl.BlockSpec((None, tm, DA_HEADS, DA_DV), lambda b, i: (b, i, 0, 0)),
            pl.BlockSpec((None, tm, D_MODEL), tok),
        ],
        out_shape=[
            jax.ShapeDtypeStruct((2, B, F, D_MODEL), BF16),
            jax.ShapeDtypeStruct((B, D_MODEL, n_valid), F32),
            jax.ShapeDtypeStruct((B, D_MODEL, F), BF16),
            jax.ShapeDtypeStruct((B, n_valid, DA_HEADS, DA_DV), F32),
            jax.ShapeDtypeStruct((B, F, D_MODEL), BF16),
        ],
        compiler_params=_params("parallel", "parallel"),
        name="da_inproj",
    )(x, g, wq, wkt, wv, c, s1, s2, ct, st)


def _da_rope_tables(pos):
    half = ROPE_DIMS // 2
    inv = ROPE_THETA ** (-jnp.arange(0, ROPE_DIMS, 2, dtype=F32) / ROPE_DIMS)
    ang = pos.astype(F32)[:, None] * inv[None, :]
    cos, sin = jnp.cos(ang), jnp.sin(ang)
    n = pos.shape[0]
    zeros = lambda w: jnp.zeros((n, w), F32)
    c64 = jnp.concatenate([cos, cos, jnp.ones((n, DA_DK - ROPE_DIMS), F32)], axis=1)
    s1_64 = jnp.concatenate([zeros(half), sin, zeros(DA_DK - ROPE_DIMS)], axis=1)
    s2_64 = jnp.concatenate([-sin, zeros(DA_DK - half)], axis=1)
    scale = DA_DK ** -0.5
    tile2 = lambda a: jnp.concatenate([a, a], axis=1) * scale
    return tile2(c64), tile2(s1_64), tile2(s2_64), cos.T, sin.T


def _da_lambda(lq1_ref, lk1_ref, lq2_ref, lk2_ref, lam_init):
    a = jnp.sum(lq1_ref[...] * lk1_ref[...], axis=-1, keepdims=True)
    b = jnp.sum(lq2_ref[...] * lk2_ref[...], axis=-1, keepdims=True)
    return jnp.exp(a) - jnp.exp(b) + lam_init


def _da_finish(acc0, l0, acc1, l1, lam, g, lam_init):
    o = acc0 / l0 - lam * (acc1 / l1)
    return _rms(o) * g * (1.0 - lam_init)


def _da_prompt_kernel(q_ref, kt_ref, v_ref, lq1_ref, lk1_ref, lq2_ref, lk2_ref, g_ref, o_ref,
                      m_sc, l_sc, acc_sc, *, tq, tk, lam_init):
    i = pl.program_id(2)
    q = q_ref[...].reshape(2 * tq, LANES)
    m_sc[...] = jnp.full(m_sc.shape, MASK_VALUE, F32)
    l_sc[...] = jnp.zeros(l_sc.shape, F32)
    acc_sc[...] = jnp.zeros(acc_sc.shape, F32)

    def step(j, masked):
        off = pl.multiple_of(j * tk, tk)
        s = jnp.dot(q, kt_ref[:, pl.ds(off, tk)], preferred_element_type=F32)
        if masked:
            row = lax.broadcasted_iota(jnp.int32, s.shape, 0)
            qpos = i * tq + jnp.where(row >= tq, row - tq, row)
            kpos = off + lax.broadcasted_iota(jnp.int32, s.shape, 1)
            s = jnp.where(kpos <= qpos, s, MASK_VALUE)
        m_old = m_sc[...]
        m_new = jnp.maximum(m_old, jnp.max(s, axis=-1, keepdims=True))
        alpha = jnp.exp(m_old - m_new)
        p = jnp.exp(s - m_new)
        l_sc[...] = alpha * l_sc[...] + jnp.sum(p, axis=-1, keepdims=True)
        acc_sc[...] = alpha * acc_sc[...] + jnp.dot(p.astype(BF16), v_ref[pl.ds(off, tk), :],
                                                    preferred_element_type=F32)
        m_sc[...] = m_new

    per_q = tq // tk
    n_full = i * per_q

    def full_body(j, carry):
        step(j, False)
        return carry

    lax.fori_loop(0, n_full, full_body, 0)
    for d in range(per_q):
        step(n_full + d, True)

    lam = _da_lambda(lq1_ref, lk1_ref, lq2_ref, lk2_ref, lam_init)
    acc, l = acc_sc[...], l_sc[...]
    o = _da_finish(acc[:tq], l[:tq], acc[tq:], l[tq:], lam, g_ref[...], lam_init)
    o_ref[...] = o.astype(o_ref.dtype)


def _da_prompt(qd, ktb, vb, lams, g, lam_init, tq, tk):
    _, B, F, _ = qd.shape
    lam_spec = pl.BlockSpec((1, DA_DK), lambda b, h, i: (0, 0))
    return pl.pallas_call(
        functools.partial(_da_prompt_kernel, tq=tq, tk=tk, lam_init=lam_init),
        grid=(B, DA_HEADS, F // tq),
        in_specs=[
            pl.BlockSpec((2, None, tq, LANES), lambda b, h, i: (0, b, i, h)),
            pl.BlockSpec((None, LANES, F), lambda b, h, i: (b, h, 0)),
            pl.BlockSpec((None, F, LANES), lambda b, h, i: (b, 0, h)),
            lam_spec, lam_spec, lam_spec, lam_spec,
            pl.BlockSpec((1, DA_DV), lambda b, h, i: (0, 0)),
        ],
        out_specs=pl.BlockSpec((None, tq, LANES), lambda b, h, i: (b, i, h)),
        out_shape=jax.ShapeDtypeStruct((B, F, D_MODEL), BF16),
        scratch_shapes=[
            pltpu.VMEM((2 * tq, 1), F32),
            pltpu.VMEM((2 * tq, 1), F32),
            pltpu.VMEM((2 * tq, LANES), F32),
        ],
        compiler_params=_params("parallel", "parallel", "parallel"),
        name="da_prompt_attn",
    )(qd, ktb, vb, *lams, g)


def _da_sample_kernel(pt_ref, q_ref, kc_ref, vc_ref, kn_ref, vn_ref, lq1_ref, lk1_ref, lq2_ref, lk2_ref,
                      g_ref, o_ref, m_sc, l_sc, acc_sc, *, n_pages, n_new, lam_init):
    del pt_ref
    p_idx = pl.program_id(1)

    @pl.when(p_idx == 0)
    def _():
        m_sc[...] = jnp.full(m_sc.shape, MASK_VALUE, F32)
        l_sc[...] = jnp.zeros(l_sc.shape, F32)
        acc_sc[...] = jnp.zeros(acc_sc.shape, F32)

    q = q_ref[...]
    for h in range(DA_HEADS):
        qh = q[:, h * LANES:(h + 1) * LANES]
        s = jnp.dot(qh, kc_ref[h].astype(BF16), preferred_element_type=F32)
        m_old = m_sc[h]
        m_new = jnp.maximum(m_old, jnp.max(s, axis=-1, keepdims=True))
        alpha = jnp.exp(m_old - m_new)
        p = jnp.exp(s - m_new)
        l_sc[h] = alpha * l_sc[h] + jnp.sum(p, axis=-1, keepdims=True)
        acc_sc[h] = alpha * acc_sc[h] + jnp.dot(p.astype(BF16), vc_ref[:, h, :].astype(BF16),
                                                preferred_element_type=F32)
        m_sc[h] = m_new

    @pl.when(p_idx == n_pages - 1)
    def _():
        lam = _da_lambda(lq1_ref, lk1_ref, lq2_ref, lk2_ref, lam_init)
        row = lax.broadcasted_iota(jnp.int32, (2 * n_new, 1), 0)
        step = jnp.where(row >= n_new, row - n_new, row)
        for h in range(DA_HEADS):
            sl = slice(h * LANES, (h + 1) * LANES)
            qh = q[:, sl].astype(F32)
            kn = kn_ref[:, sl]
            s_new = [jnp.where(step >= j, jnp.sum(qh * kn[j:j + 1, :], axis=-1, keepdims=True), MASK_VALUE)
                     for j in range(n_new)]
            m_old = m_sc[h]
            m_new = functools.reduce(jnp.maximum, s_new, m_old)
            alpha = jnp.exp(m_old - m_new)
            l = alpha * l_sc[h]
            acc = alpha * acc_sc[h]
            for j in range(n_new):
                pj = jnp.exp(s_new[j] - m_new)
                l = l + pj
                acc = acc + pj * vn_ref[j:j + 1, h, :]
            o = _da_finish(acc[:n_new], l[:n_new], acc[n_new:], l[n_new:], lam, g_ref[...], lam_init)
            o_ref[:, sl] = o


def _da_sample(page_table, layer, q2, kc, vc, kn, vn, lams, g, lam_init):
    DB, n_pages = page_table.shape
    T = kn.shape[1]
    page = vc.shape[2]
    lam_spec = pl.BlockSpec((1, DA_DK), lambda b, p, pt: (0, 0))
    grid_spec = pltpu.PrefetchScalarGridSpec(
        num_scalar_prefetch=1,
        grid=(DB, n_pages),
        in_specs=[
            pl.BlockSpec((None, 2 * T, D_MODEL), lambda b, p, pt: (b, 0, 0)),
            pl.BlockSpec((None, None, DA_HEADS, LANES, page), lambda b, p, pt: (layer, pt[b, p], 0, 0, 0)),
            pl.BlockSpec((None, None, page, DA_HEADS, DA_DV), lambda b, p, pt: (layer, pt[b, p], 0, 0, 0)),
            pl.BlockSpec((None, T, D_MODEL), lambda b, p, pt: (b, 0, 0)),
            pl.BlockSpec((None, T, DA_HEADS, DA_DV), lambda b, p, pt: (b, 0, 0, 0)),
            lam_spec, lam_spec, lam_spec, lam_spec,
            pl.BlockSpec((1, DA_DV), lambda b, p, pt: (0, 0)),
        ],
        out_specs=pl.BlockSpec((None, T, D_MODEL), lambda b, p, pt: (b, 0, 0)),
        scratch_shapes=[
            pltpu.VMEM((DA_HEADS, 2 * T, 1), F32),
            pltpu.VMEM((DA_HEADS, 2 * T, 1), F32),
            pltpu.VMEM((DA_HEADS, 2 * T, DA_DV), F32),
        ],
    )
    return pl.pallas_call(
        functools.partial(_da_sample_kernel, n_pages=n_pages, n_new=T, lam_init=lam_init),
        grid_spec=grid_spec,
        out_shape=jax.ShapeDtypeStruct((DB, T, D_MODEL), F32),
        compiler_params=_params("parallel", "arbitrary"),
        name="da_sample_attn",
    )(page_table, q2, kc, vc, kn, vn, *lams, g)


def _ffn_kernel(x_ref, o_ref, wo_ref, g_ref, win_ref, wout_ref, gout_ref, y_ref, acc_sc, *, final_norm):
    h = x_ref[...] + jnp.dot(o_ref[...].astype(BF16), wo_ref[...], preferred_element_type=F32)
    hn = (_rms(h) * g_ref[...]).astype(BF16)
    acc_sc[...] = h
    for j in range(FFN_HIDDEN // FFN_CHUNK):
        lo = j * FFN_CHUNK
        gate = jnp.dot(hn, win_ref[:, lo:lo + FFN_CHUNK], preferred_element_type=F32)
        up = jnp.dot(hn, win_ref[:, FFN_HIDDEN + lo:FFN_HIDDEN + lo + FFN_CHUNK], preferred_element_type=F32)
        act = (gate * jax.nn.sigmoid(gate) * up).astype(BF16)
        acc_sc[...] += jnp.dot(act, wout_ref[lo:lo + FFN_CHUNK, :], preferred_element_type=F32)
    y = acc_sc[...]
    if final_norm:
        y = _rms(y) * gout_ref[...]
    y_ref[...] = y


def _outproj_ffn(x, o, wo, g, win, wout, gout, tm, final_norm):
    N = x.shape[0]
    ko = o.shape[1]
    return pl.pallas_call(
        functools.partial(_ffn_kernel, final_norm=final_norm),
        grid=(N // tm,),
        in_specs=[
            pl.BlockSpec((tm, D_MODEL), lambda i: (i, 0)),
            pl.BlockSpec((tm, ko), lambda i: (i, 0)),
            _resident((ko, D_MODEL)),
            _resident((1, D_MODEL)),
            _resident((D_MODEL, 2 * FFN_HIDDEN)),
            _resident((FFN_HIDDEN, D_MODEL)),
            _resident((1, D_MODEL)),
        ],
        out_specs=pl.BlockSpec((tm, D_MODEL), lambda i: (i, 0)),
        out_shape=jax.ShapeDtypeStruct((N, D_MODEL), F32),
        scratch_shapes=[pltpu.VMEM((tm, D_MODEL), F32)],
        compiler_params=_params("parallel"),
        name="outproj_ffn",
    )(x, o, wo, g, win, wout, gout)


def _ret_inproj_kernel(x_ref, g_ref, wq_ref, wkt_ref, wv_ref, wg_ref, c_ref, a_ref, b_ref,
                       ct_ref, at_ref, bt_ref, q_ref, kt_ref, v_ref, sg_ref):
    hn = (_rms(x_ref[...]) * g_ref[...]).astype(BF16)
    q = jnp.dot(hn, wq_ref[...], preferred_element_type=F32)
    c, a, b = c_ref[...], a_ref[...], b_ref[...]
    for h in range(RET_HEADS):
        sl = slice(h * RET_DK, (h + 1) * RET_DK)
        qh = q[:, sl]
        qr = qh * c + pltpu.roll(qh, RET_DK - 1, 1) * a + pltpu.roll(qh, 1, 1) * b
        q_ref[:, sl] = qr.astype(BF16)
    kt = lax.dot_general(wkt_ref[...], hn, _NT, preferred_element_type=F32)
    ct, at, bt = ct_ref[...], at_ref[...], bt_ref[...]
    for h in range(RET_HEADS):
        sl = slice(h * RET_DK, (h + 1) * RET_DK)
        kh = kt[sl]
        kr = kh * ct + pltpu.roll(kh, RET_DK - 1, 0) * at + pltpu.roll(kh, 1, 0) * bt
        kt_ref[sl, :] = kr.astype(BF16)
    v_ref[...] = jnp.dot(hn, wv_ref[...], preferred_element_type=F32).astype(BF16)
    gate = jnp.dot(hn, wg_ref[...], preferred_element_type=F32)
    sg_ref[...] = (gate * jax.nn.sigmoid(gate)).astype(BF16)


def _ret_inproj(x, g, wq, wkt, wv, wg, tabs, tm):
    B, F, _ = x.shape
    c, a, b, ct, at, bt = tabs
    tok = lambda bb, i: (bb, i, 0)
    tab = pl.BlockSpec((tm, RET_DK), lambda bb, i: (i, 0))
    tab_t = pl.BlockSpec((RET_DK, tm), lambda bb, i: (0, i))
    return pl.pallas_call(
        _ret_inproj_kernel,
        grid=(B, F // tm),
        in_specs=[
            pl.BlockSpec((None, tm, D_MODEL), tok),
            _resident((1, D_MODEL)),
            _resident((D_MODEL, RET_QK)),
            _resident((RET_QK, D_MODEL)),
            _resident((D_MODEL, RET_VD)),
            _resident((D_MODEL, RET_VD)),
            tab, tab, tab, tab_t, tab_t, tab_t,
        ],
        out_specs=[
            pl.BlockSpec((None, tm, RET_QK), tok),
            pl.BlockSpec((None, RET_QK, tm), lambda bb, i: (bb, 0, i)),
            pl.BlockSpec((None, tm, RET_VD), tok),
            pl.BlockSpec((None, tm, RET_VD), tok),
        ],
        out_shape=[
            jax.ShapeDtypeStruct((B, F, RET_QK), BF16),
            jax.ShapeDtypeStruct((B, RET_QK, F), BF16),
            jax.ShapeDtypeStruct((B, F, RET_VD), BF16),
            jax.ShapeDtypeStruct((B, F, RET_VD), BF16),
        ],
        compiler_params=_params("parallel", "parallel"),
        name="ret_inproj",
    )(x, g, wq, wkt, wv, wg, c, a, b, ct, at, bt)


def _ret_rot_tables(pos):
    angle = 1.0 / (RET_THETA ** jnp.linspace(0.0, 1.0, RET_DK // 2, dtype=F32))
    angle = jnp.repeat(angle, 2)
    ang = pos.astype(F32)[:, None] * angle[None, :]
    cos, sin = jnp.cos(ang), jnp.sin(ang)
    even = (jnp.arange(RET_DK) % 2 == 0)[None, :]
    a = jnp.where(even, -sin, 0.0)
    b = jnp.where(even, 0.0, sin)
    ks = RET_DK ** -0.5
    return cos, a, b, (cos * ks).T, (a * ks).T, (b * ks).T


def _ret_log_decay(h):
    return math.log(1.0 - 2.0 ** (-5.0 - h))


def _ret_gate_out(o, sg):
    return (sg.astype(F32) * _rms(o)).astype(BF16)


def _ret_prompt_kernel(q_ref, kt_ref, v_ref, sg_ref, og_ref, st_ref, s_sc, *, n_valid):
    c_idx = pl.program_id(1)

    @pl.when(c_idx == 0)
    def _():
        s_sc[...] = jnp.zeros(s_sc.shape, F32)

    C = RET_CHUNK
    valid = jnp.clip(n_valid - c_idx * C, 0, C).astype(F32)
    valid_v = jnp.full((1, 1), valid, F32)
    n_row = lax.broadcasted_iota(jnp.int32, (C, 1), 0).astype(F32)
    n_col = lax.broadcasted_iota(jnp.int32, (1, C), 1).astype(F32)
    diff = n_row - n_col
    for h in range(RET_HEADS):
        lg = _ret_log_decay(h)
        q = q_ref[:, h * RET_DK:(h + 1) * RET_DK]
        kt = kt_ref[h * RET_DK:(h + 1) * RET_DK, :]
        v = v_ref[:, h * RET_DV:(h + 1) * RET_DV]
        dmask = jnp.where(diff >= 0, jnp.exp(lg * jnp.maximum(diff, 0.0)), 0.0)
        s = jnp.dot(q, kt, preferred_element_type=F32) * dmask
        inner = jnp.dot(s.astype(BF16), v, preferred_element_type=F32)
        state = s_sc[h]
        q_dec = (q.astype(F32) * jnp.exp(lg * (n_row + 1.0))).astype(BF16)
        cross = jnp.dot(q_dec, state.astype(BF16), preferred_element_type=F32)
        k_decay = jnp.where(n_col < valid, jnp.exp(lg * (valid - 1.0 - n_col)), 0.0)
        k_dec = (kt.astype(F32) * k_decay).astype(BF16)
        s_sc[h] = jnp.exp(lg * valid_v) * state + jnp.dot(k_dec, v, preferred_element_type=F32)
        sl = slice(h * RET_DV, (h + 1) * RET_DV)
        og_ref[:, sl] = _ret_gate_out(inner + cross, sg_ref[:, sl])

    @pl.when(c_idx == pl.num_programs(1) - 1)
    def _():
        st_ref[...] = s_sc[...]


def _ret_prompt(q, kt, v, sg, n_valid):
    B, F, _ = q.shape
    C = RET_CHUNK
    tok = lambda b, c: (b, c, 0)
    return pl.pallas_call(
        functools.partial(_ret_prompt_kernel, n_valid=n_valid),
        grid=(B, F // C),
        in_specs=[
            pl.BlockSpec((None, C, RET_QK), tok),
            pl.BlockSpec((None, RET_QK, C), lambda b, c: (b, 0, c)),
            pl.BlockSpec((None, C, RET_VD), tok),
            pl.BlockSpec((None, C, RET_VD), tok),
        ],
        out_specs=[
            pl.BlockSpec((None, C, RET_VD), tok),
            pl.BlockSpec((None, RET_HEADS, RET_DK, RET_DV), lambda b, c: (b, 0, 0, 0)),
        ],
        out_shape=[
            jax.ShapeDtypeStruct((B, F, RET_VD), BF16),
            jax.ShapeDtypeStruct((B, RET_HEADS, RET_DK, RET_DV), F32),
        ],
        scratch_shapes=[pltpu.VMEM((RET_HEADS, RET_DK, RET_DV), F32)],
        compiler_params=_params("parallel", "arbitrary"),
        name="ret_prompt",
    )(q, kt, v, sg)


def _ret_sample_kernel(q_ref, k_ref, v_ref, sg_ref, st_in_ref, og_ref, st_out_ref, *, n_new):
    T = n_new
    n_row = lax.broadcasted_iota(jnp.int32, (T, 1), 0).astype(F32)
    n_col = lax.broadcasted_iota(jnp.int32, (1, T), 1).astype(F32)
    diff = n_row - n_col
    for h in range(RET_HEADS):
        lg = _ret_log_decay(h)
        q = q_ref[:, h * RET_DK:(h + 1) * RET_DK].astype(F32)
        k = k_ref[:, h * RET_DK:(h + 1) * RET_DK].astype(F32)
        v = v_ref[:, h * RET_DV:(h + 1) * RET_DV].astype(F32)
        dmask = jnp.where(diff >= 0, jnp.exp(lg * jnp.maximum(diff, 0.0)), 0.0)
        s = lax.dot_general(q, k, _NT, preferred_element_type=F32) * dmask
        inner = jnp.dot(s, v, preferred_element_type=F32)
        state = st_in_ref[h]
        q_dec = (q * jnp.exp(lg * (n_row + 1.0))).astype(BF16)
        cross = jnp.dot(q_dec, state.astype(BF16), preferred_element_type=F32)
        k_dec = k * jnp.exp(lg * (T - 1.0 - n_row))
        st_out_ref[h] = math.exp(lg * T) * state + lax.dot_general(k_dec, v, _TN, preferred_element_type=F32)
        sl = slice(h * RET_DV, (h + 1) * RET_DV)
        og_ref[:, sl] = _ret_gate_out(inner + cross, sg_ref[:, sl])


def _ret_sample(q, k, v, sg, state):
    DB, T, _ = q.shape
    tok = lambda b: (b, 0, 0)
    st = pl.BlockSpec((None, RET_HEADS, RET_DK, RET_DV), lambda b: (b, 0, 0, 0))
    return pl.pallas_call(
        functools.partial(_ret_sample_kernel, n_new=T),
        grid=(DB,),
        in_specs=[
            pl.BlockSpec((None, T, RET_QK), tok),
            pl.BlockSpec((None, T, RET_QK), tok),
            pl.BlockSpec((None, T, RET_VD), tok),
            pl.BlockSpec((None, T, RET_VD), tok),
            st,
        ],
        out_specs=[pl.BlockSpec((None, T, RET_VD), tok), st],
        out_shape=[
            jax.ShapeDtypeStruct((DB, T, RET_VD), BF16),
            jax.ShapeDtypeStruct((DB, RET_HEADS, RET_DK, RET_DV), F32),
        ],
        compiler_params=_params("parallel"),
        name="ret_sample",
    )(q, k, v, sg, state)


def kernel(x_prompt, x_sample, cache_k, cache_v, state_ret, page_table, meta_tokens, norm_mix, norm_ffn, norm_out, da_w_in, da_w_out, da_lambda_q1, da_lambda_k1, da_lambda_q2, da_lambda_k2, da_subln, ret_w_in, ret_w_out, ffn_w_in, ffn_w_out):
    B, seq, _ = x_prompt.shape
    DB, T, _ = x_sample.shape
    depth = norm_mix.shape[0]
    L = N_META + seq
    NS = DB * T
    page = cache_k.shape[2]
    past = page_table.shape[1] * page

    tp = _tiles(L)
    ts = {k: min(t, NS) for k, t in _tiles(NS).items()}
    F = _round_up(L, tp["frame"])

    meta = jnp.broadcast_to(meta_tokens.astype(F32)[None], (B, N_META, D_MODEL))
    hp = jnp.concatenate([meta, x_prompt, jnp.zeros((B, F - L, D_MODEL), F32)], axis=1)
    hs = x_sample.reshape(1, NS, D_MODEL)

    pos_p = jnp.arange(F)
    pos_s = past + jnp.arange(NS) % T
    da_tabs_p, da_tabs_s = _da_rope_tables(pos_p), _da_rope_tables(pos_s)
    ret_tabs_p, ret_tabs_s = _ret_rot_tables(pos_p), _ret_rot_tables(pos_s)

    kc = jnp.transpose(cache_k, (0, 1, 3, 4, 5, 2)).reshape(cache_k.shape[0], cache_k.shape[1], DA_HEADS, 2 * DA_DK, page)
    vc = cache_v

    row = lambda a: a.astype(F32).reshape(1, -1)
    kp_rows, vp_rows, ks_rows, vs_rows, st_p_list, st_s_list = [], [], [], [], [], []
    for i in range(depth):
        j = i // N_MIXERS
        g_mix = row(norm_mix[i])
        if i % N_MIXERS == 0:
            lam_init = 0.8 - 0.6 * math.exp(-0.3 * i)
            w = da_w_in[j]
            wq = w[:, :D_MODEL].astype(BF16)
            wkt = w[:, D_MODEL:2 * D_MODEL].T.astype(BF16)
            wv = w[:, 2 * D_MODEL:].astype(BF16)
            lams = (row(da_lambda_q1[j]), row(da_lambda_k1[j]), row(da_lambda_q2[j]), row(da_lambda_k2[j]))
            g_sub = row(da_subln[j])
            qd, ktf, ktb, vf, vb = _da_inproj(hp, g_mix, wq, wkt, wv, da_tabs_p, L, tp["tm_proj"])
            qd_s, ktf_s, _, vf_s, _ = _da_inproj(hs, g_mix, wq, wkt, wv, da_tabs_s, NS, ts["tm_proj"])
            op = _da_prompt(qd, ktb, vb, lams, g_sub, lam_init, tp["tq"], tp["tk"])
            q2 = qd_s.reshape(2, DB, T, D_MODEL).transpose(1, 0, 2, 3).reshape(DB, 2 * T, D_MODEL)
            kn = ktf_s[0].T.reshape(DB, T, D_MODEL)
            vn = vf_s.reshape(DB, T, DA_HEADS, DA_DV)
            osm = _da_sample(page_table, j, q2, kc, vc, kn, vn, lams, g_sub, lam_init)
            o_p, o_s = op.reshape(B * F, D_MODEL), osm.reshape(NS, D_MODEL)
            wo = da_w_out[j].astype(BF16)
            kp_rows.append(ktf.reshape(B, DA_HEADS, 2, DA_DK, L).transpose(0, 4, 1, 2, 3))
            vp_rows.append(vf)
            ks_rows.append(ktf_s.reshape(DA_HEADS, 2, DA_DK, DB, T).transpose(3, 4, 0, 1, 2))
            vs_rows.append(vn)
        else:
            w = ret_w_in[j]
            wq = w[:, :RET_QK].astype(BF16)
            wkt = w[:, RET_QK:2 * RET_QK].T.astype(BF16)
            wv = w[:, 2 * RET_QK:2 * RET_QK + RET_VD].astype(BF16)
            wg = w[:, 2 * RET_QK + RET_VD:].astype(BF16)
            q, kt, v, sg = _ret_inproj(hp, g_mix, wq, wkt, wv, wg, ret_tabs_p, tp["tm_proj"])
            q_s, kt_s, v_s, sg_s = _ret_inproj(hs, g_mix, wq, wkt, wv, wg, ret_tabs_s, ts["tm_proj"])
            og, st_p = _ret_prompt(q, kt, v, sg, L)
            per_seq = lambda a: a.reshape(DB, T, a.shape[-1])
            og_s, st_s = _ret_sample(per_seq(q_s[0]), per_seq(kt_s[0].T), per_seq(v_s[0]), per_seq(sg_s[0]),
                                     state_ret[j])
            o_p, o_s = og.reshape(B * F, RET_VD), og_s.reshape(NS, RET_VD)
            wo = ret_w_out[j].astype(BF16)
            st_p_list.append(st_p)
            st_s_list.append(st_s)
        last = i == depth - 1
        ffn = (wo, row(norm_ffn[i]), ffn_w_in[i].astype(BF16), ffn_w_out[i].astype(BF16), row(norm_out))
        hp = _outproj_ffn(hp.reshape(B * F, D_MODEL), o_p, *ffn, tp["tm_ffn"], last).reshape(B, F, D_MODEL)
        hs = _outproj_ffn(hs.reshape(NS, D_MODEL), o_s, *ffn, ts["tm_ffn"], last).reshape(1, NS, D_MODEL)
    y_prompt = hp[:, N_META:L]
    y_sample = hs.reshape(DB, T, D_MODEL)
    return (y_prompt, y_sample, jnp.stack(kp_rows), jnp.stack(vp_rows), jnp.stack(ks_rows),
            jnp.stack(vs_rows), jnp.stack(st_p_list), jnp.stack(st_s_list))
```

```python
import functools
import math

import jax
import jax.numpy as jnp
from jax import lax
from jax.experimental import pallas as pl
from jax.experimental.pallas import tpu as pltpu

F32, BF16 = jnp.float32, jnp.bfloat16

D_MODEL = 1024
N_META = 16
N_MIXERS = 2
DA_HEADS = 8
DA_DK = 64
DA_DV = 128
ROPE_THETA = 500000.0
ROPE_DIMS = 16
RET_HEADS = 4
RET_DK = 256
RET_DV = 512
RET_QK = RET_HEADS * RET_DK
RET_VD = RET_HEADS * RET_DV
RET_CHUNK = 128
RET_THETA = 10000.0
FFN_HIDDEN = 2816
FFN_CHUNK = 256
RMS_EPS = 1e-6
MASK_VALUE = -1e30
LANES = 128
VMEM_LIMIT = 56 * 1024 * 1024

_NT = (((1,), (1,)), ((), ()))
_TN = (((0,), (0,)), ((), ()))


def _tiles(n_rows):
    if n_rows >= 4096:
        return dict(frame=768, tm_proj=384, tm_ffn=768, tq=768, tk=768)
    if n_rows >= 512:
        return dict(frame=512, tm_proj=512, tm_ffn=512, tq=512, tk=256)
    return dict(frame=256, tm_proj=128, tm_ffn=128, tq=256, tk=128)


def _round_up(x, m):
    return -(-x // m) * m


def _rms(x):
    return x * lax.rsqrt(jnp.mean(x * x, axis=-1, keepdims=True) + RMS_EPS)


def _resident(shape):
    return pl.BlockSpec(shape, lambda *_: (0,) * len(shape), pipeline_mode=pl.Buffered(1))


def _params(*sem):
    return pltpu.CompilerParams(dimension_semantics=sem, vmem_limit_bytes=VMEM_LIMIT)


def _da_inproj_kernel(x_ref, g_ref, wq_ref, wkt_ref, wv_ref, c_ref, s1_ref, s2_ref, ct_ref, st_ref,
                      qd_ref, ktf_ref, ktb_ref, vf_ref, vb_ref):
    hn = (_rms(x_ref[...]) * g_ref[...]).astype(BF16)
    lane = lax.broadcasted_iota(jnp.int32, (1, LANES), 1)
    comp0 = (lane < DA_DK).astype(F32)
    comp1 = 1.0 - comp0
    q = jnp.dot(hn, wq_ref[...], preferred_element_type=F32)
    c, s1, s2 = c_ref[...], s1_ref[...], s2_ref[...]
    for h in range(DA_HEADS):
        sl = slice(h * LANES, (h + 1) * LANES)
        qh = q[:, sl]
        qr = qh * c + pltpu.roll(qh, ROPE_DIMS // 2, 1) * s1 + pltpu.roll(qh, LANES - ROPE_DIMS // 2, 1) * s2
        qd_ref[0, :, sl] = (qr * comp0).astype(BF16)
        qd_ref[1, :, sl] = (qr * comp1).astype(BF16)
    kt = lax.dot_general(wkt_ref[...], hn, _NT, preferred_element_type=F32)
    ct, st = ct_ref[...], st_ref[...]
    half = ROPE_DIMS // 2
    for grp in range(2 * DA_HEADS):
        b0 = grp * DA_DK
        x1 = kt[b0:b0 + half]
        x2 = kt[b0 + half:b0 + 2 * half]
        rest = kt[b0 + 2 * half:b0 + DA_DK]
        o1 = x1 * ct - x2 * st
        o2 = x2 * ct + x1 * st
        for lo, val in ((b0, o1), (b0 + half, o2), (b0 + 2 * half, rest)):
            ktf_ref[lo:lo + val.shape[0], :] = val
            ktb_ref[lo:lo + val.shape[0], :] = val.astype(BF16)
    v = jnp.dot(hn, wv_ref[...], preferred_element_type=F32)
    vb_ref[...] = v.astype(BF16)
    for h in range(DA_HEADS):
        vf_ref[:, h, :] = v[:, h * DA_DV:(h + 1) * DA_DV]


def _drop_refs(body, start, count):
    def wrapped(*refs):
        return body(*refs[:start], *refs[start + count:])
    return wrapped


def _da_inproj(x, g, wq, wkt, wv, tabs, n_valid, tm):
    B, F, _ = x.shape
    c, s1, s2, ct, st = tabs
    tok = lambda b, i: (b, i, 0)
    return pl.pallas_call(
        _da_inproj_kernel,
        grid=(B, F // tm),
        in_specs=[
            pl.BlockSpec((None, tm, D_MODEL), tok),
            _resident((1, D_MODEL)),
            _resident((D_MODEL, D_MODEL)),
            _resident((D_MODEL, D_MODEL)),
            _resident((D_MODEL, D_MODEL)),
            pl.BlockSpec((tm, LANES), lambda b, i: (i, 0)),
            pl.BlockSpec((tm, LANES), lambda b, i: (i, 0)),
            pl.BlockSpec((tm, LANES), lambda b, i: (i, 0)),
            pl.BlockSpec((ROPE_DIMS // 2, tm), lambda b, i: (0, i)),
            pl.BlockSpec((ROPE_DIMS // 2, tm), lambda b, i: (0, i)),
        ],
        out_specs=[
            pl.BlockSpec((2, None, tm, D_MODEL), lambda b, i: (0, b, i, 0)),
            pl.BlockSpec((None, D_MODEL, tm), lambda b, i: (b, 0, i)),
            pl.BlockSpec((None, D_MODEL, tm), lambda b, i: (b, 0, i)),
            pl.BlockSpec((None, tm, DA_HEADS, DA_DV), lambda b, i: (b, i, 0, 0)),
            pl.BlockSpec((None, tm, D_MODEL), tok),
        ],
        out_shape=[
            jax.ShapeDtypeStruct((2, B, F, D_MODEL), BF16),
            jax.ShapeDtypeStruct((B, D_MODEL, n_valid), F32),
            jax.ShapeDtypeStruct((B, D_MODEL, F), BF16),
            jax.ShapeDtypeStruct((B, n_valid, DA_HEADS, DA_DV), F32),
            jax.ShapeDtypeStruct((B, F, D_MODEL), BF16),
        ],
        compiler_params=_params("parallel", "parallel"),
        name="da_inproj",
    )(x, g, wq, wkt, wv, c, s1, s2, ct, st)


def _da_rope_tables(pos):
    half = ROPE_DIMS // 2
    inv = ROPE_THETA ** (-jnp.arange(0, ROPE_DIMS, 2, dtype=F32) / ROPE_DIMS)
    ang = pos.astype(F32)[:, None] * inv[None, :]
    cos, sin = jnp.cos(ang), jnp.sin(ang)
    n = pos.shape[0]
    zeros = lambda w: jnp.zeros((n, w), F32)
    c64 = jnp.concatenate([cos, cos, jnp.ones((n, DA_DK - ROPE_DIMS), F32)], axis=1)
    s1_64 = jnp.concatenate([zeros(half), sin, zeros(DA_DK - ROPE_DIMS)], axis=1)
    s2_64 = jnp.concatenate([-sin, zeros(DA_DK - half)], axis=1)
    scale = DA_DK ** -0.5 * math.log2(math.e)
    tile2 = lambda a: jnp.concatenate([a, a], axis=1) * scale
    return tile2(c64), tile2(s1_64), tile2(s2_64), cos.T, sin.T


def _da_lambda(lq1_ref, lk1_ref, lq2_ref, lk2_ref, lam_init):
    a = jnp.sum(lq1_ref[...] * lk1_ref[...], axis=-1, keepdims=True)
    b = jnp.sum(lq2_ref[...] * lk2_ref[...], axis=-1, keepdims=True)
    return jnp.exp(a) - jnp.exp(b) + lam_init


def _da_finish(acc0, l0, acc1, l1, lam, g, lam_init):
    o = acc0 / l0 - lam * (acc1 / l1)
    return _rms(o) * g * (1.0 - lam_init)


def _da_prompt_kernel(q_ref, kt_ref, v_ref, lq1_ref, lk1_ref, lq2_ref, lk2_ref, g_ref, o_ref,
                      m_sc, l_sc, acc_sc, *, tq, tk, lam_init):
    i = pl.program_id(2)
    m_sc[...] = jnp.full(m_sc.shape, MASK_VALUE, F32)
    l_sc[...] = jnp.zeros(l_sc.shape, F32)
    acc_sc[...] = jnp.zeros(acc_sc.shape, F32)
    n_tiles = tk // LANES

    def step(j, masked):
        off = pl.multiple_of(j * tk, tk)
        kt = kt_ref[:, pl.ds(off, tk)]
        v = v_ref[pl.ds(off, tk), :]
        for c in range(2):
            s = jnp.dot(q_ref[c], kt, preferred_element_type=F32)
            if masked:
                qpos = i * tq + lax.broadcasted_iota(jnp.int32, s.shape, 0)
                kpos = off + lax.broadcasted_iota(jnp.int32, s.shape, 1)
                s = jnp.where(kpos <= qpos, s, MASK_VALUE)
            tiles = [s[:, t * LANES:(t + 1) * LANES] for t in range(n_tiles)]
            m_old = m_sc[c]
            blk_max = jnp.max(functools.reduce(jnp.maximum, tiles), axis=-1, keepdims=True)
            m_new = jnp.maximum(m_old, blk_max)
            alpha = jnp.exp2(m_old - m_new)
            ps = [jnp.exp2(t - m_new) for t in tiles]
            l_sc[c] = alpha * l_sc[c] + functools.reduce(jnp.add, ps)
            p = jnp.concatenate(ps, axis=1).astype(BF16)
            acc_sc[c] = alpha * acc_sc[c] + jnp.dot(p, v, preferred_element_type=F32)
            m_sc[c] = m_new

    per_q = tq // tk
    n_full = i * per_q

    def full_body(j, carry):
        step(j, False)
        return carry

    lax.fori_loop(0, n_full, full_body, 0)
    for d in range(per_q):
        step(n_full + d, True)

    lam = _da_lambda(lq1_ref, lk1_ref, lq2_ref, lk2_ref, lam_init)
    l0 = jnp.sum(l_sc[0], axis=-1, keepdims=True)
    l1 = jnp.sum(l_sc[1], axis=-1, keepdims=True)
    o = _da_finish(acc_sc[0], l0, acc_sc[1], l1, lam, g_ref[...], lam_init)
    o_ref[...] = o.astype(o_ref.dtype)


def _da_prompt(qd, ktb, vb, lams, g, lam_init, tq, tk):
    _, B, F, _ = qd.shape
    lam_spec = pl.BlockSpec((1, DA_DK), lambda b, h, i: (0, 0))
    return pl.pallas_call(
        functools.partial(_da_prompt_kernel, tq=tq, tk=tk, lam_init=lam_init),
        grid=(B, DA_HEADS, F // tq),
        in_specs=[
            pl.BlockSpec((2, None, tq, LANES), lambda b, h, i: (0, b, i, h)),
            pl.BlockSpec((None, LANES, F), lambda b, h, i: (b, h, 0)),
            pl.BlockSpec((None, F, LANES), lambda b, h, i: (b, 0, h)),
            lam_spec, lam_spec, lam_spec, lam_spec,
            pl.BlockSpec((1, DA_DV), lambda b, h, i: (0, 0)),
        ],
        out_specs=pl.BlockSpec((None, tq, LANES), lambda b, h, i: (b, i, h)),
        out_shape=jax.ShapeDtypeStruct((B, F, D_MODEL), BF16),
        scratch_shapes=[
            pltpu.VMEM((2, tq, LANES), F32),
            pltpu.VMEM((2, tq, LANES), F32),
            pltpu.VMEM((2, tq, LANES), F32),
        ],
        compiler_params=_params("parallel", "parallel", "parallel"),
        name="da_prompt_attn",
    )(qd, ktb, vb, *lams, g)


def _da_sample_kernel(pt_ref, q_ref, kn_ref, vn_ref, lq1_ref, lk1_ref, lq2_ref, lk2_ref, g_ref, *rest,
                      n_pages, n_new, lam_init):
    del pt_ref
    kc_refs, vc_refs, o_ref = rest[:n_pages], rest[n_pages:2 * n_pages], rest[2 * n_pages]
    page = vc_refs[0].shape[0] // DA_HEADS
    lam = _da_lambda(lq1_ref, lk1_ref, lq2_ref, lk2_ref, lam_init)
    q = q_ref[...]
    row = lax.broadcasted_iota(jnp.int32, (2 * n_new, 1), 0)
    step = jnp.where(row >= n_new, row - n_new, row)
    for h in range(DA_HEADS):
        sl = slice(h * LANES, (h + 1) * LANES)
        qh = q[:, sl]
        kt = jnp.concatenate([r[h].astype(BF16) for r in kc_refs], axis=1)
        s = jnp.dot(qh, kt, preferred_element_type=F32)
        tiles = [s[:, t * LANES:(t + 1) * LANES] for t in range(s.shape[1] // LANES)]
        qf = qh.astype(F32)
        kn = kn_ref[:, sl]
        s_new = [jnp.where(step >= j, jnp.sum(qf * kn[j:j + 1, :], axis=-1, keepdims=True), MASK_VALUE)
                 for j in range(n_new)]
        m = jnp.max(functools.reduce(jnp.maximum, tiles), axis=-1, keepdims=True)
        m = functools.reduce(jnp.maximum, s_new, m)
        ps = [jnp.exp2(t - m) for t in tiles]
        l = jnp.sum(functools.reduce(jnp.add, ps), axis=-1, keepdims=True)
        v = jnp.concatenate([r[pl.ds(h, page, stride=DA_HEADS), :].astype(BF16) for r in vc_refs], axis=0)
        acc = jnp.dot(jnp.concatenate(ps, axis=1).astype(BF16), v, preferred_element_type=F32)
        for j in range(n_new):
            pj = jnp.exp2(s_new[j] - m)
            l = l + pj
            acc = acc + pj * vn_ref[j:j + 1, h, :]
        o_ref[:, sl] = _da_finish(acc[:n_new], l[:n_new], acc[n_new:], l[n_new:], lam, g_ref[...], lam_init)


def _da_sample(page_table, layer, q2, kc, vc, kn, vn, lams, g, lam_init):
    DB, n_pages = page_table.shape
    T = kn.shape[1]
    page = kc.shape[-1]
    lam_spec = pl.BlockSpec((1, DA_DK), lambda b, pt: (0, 0))
    k_spec = lambda r: pl.BlockSpec((None, None, DA_HEADS, LANES, page), lambda b, pt: (layer, pt[b, r], 0, 0, 0))
    v_spec = lambda r: pl.BlockSpec((None, None, page * DA_HEADS, DA_DV), lambda b, pt: (layer, pt[b, r], 0, 0))
    grid_spec = pltpu.PrefetchScalarGridSpec(
        num_scalar_prefetch=1,
        grid=(DB,),
        in_specs=[
            pl.BlockSpec((None, 2 * T, D_MODEL), lambda b, pt: (b, 0, 0)),
            pl.BlockSpec((None, T, D_MODEL), lambda b, pt: (b, 0, 0)),
            pl.BlockSpec((None, T, DA_HEADS, DA_DV), lambda b, pt: (b, 0, 0, 0)),
            lam_spec, lam_spec, lam_spec, lam_spec,
            pl.BlockSpec((1, DA_DV), lambda b, pt: (0, 0)),
        ] + [k_spec(r) for r in range(n_pages)] + [v_spec(r) for r in range(n_pages)],
        out_specs=pl.BlockSpec((None, T, D_MODEL), lambda b, pt: (b, 0, 0)),
    )
    return pl.pallas_call(
        functools.partial(_da_sample_kernel, n_pages=n_pages, n_new=T, lam_init=lam_init),
        grid_spec=grid_spec,
        out_shape=jax.ShapeDtypeStruct((DB, T, D_MODEL), F32),
        compiler_params=_params("parallel"),
        name="da_sample_attn",
    )(page_table, q2, kn, vn, *lams, g, *([kc] * n_pages), *([vc] * n_pages))


def _ffn_kernel(x_ref, o_ref, wo_ref, g_ref, win_ref, wout_ref, gout_ref, y_ref, acc_sc, *, final_norm):
    h = x_ref[...] + jnp.dot(o_ref[...].astype(BF16), wo_ref[...], preferred_element_type=F32)
    hn = (_rms(h) * g_ref[...]).astype(BF16)
    acc_sc[...] = h
    for j in range(FFN_HIDDEN // FFN_CHUNK):
        lo = j * FFN_CHUNK
        gate = jnp.dot(hn, win_ref[:, lo:lo + FFN_CHUNK], preferred_element_type=F32)
        up = jnp.dot(hn, win_ref[:, FFN_HIDDEN + lo:FFN_HIDDEN + lo + FFN_CHUNK], preferred_element_type=F32)
        act = (gate * jax.nn.sigmoid(gate) * up).astype(BF16)
        acc_sc[...] += jnp.dot(act, wout_ref[lo:lo + FFN_CHUNK, :], preferred_element_type=F32)
    y = acc_sc[...]
    if final_norm:
        y = _rms(y) * gout_ref[...]
    y_ref[...] = y


def _outproj_ffn(x, o, wo, g, win, wout, gout, tm, final_norm):
    N = x.shape[0]
    ko = o.shape[1]
    return pl.pallas_call(
        functools.partial(_ffn_kernel, final_norm=final_norm),
        grid=(N // tm,),
        in_specs=[
            pl.BlockSpec((tm, D_MODEL), lambda i: (i, 0)),
            pl.BlockSpec((tm, ko), lambda i: (i, 0)),
            _resident((ko, D_MODEL)),
            _resident((1, D_MODEL)),
            _resident((D_MODEL, 2 * FFN_HIDDEN)),
            _resident((FFN_HIDDEN, D_MODEL)),
            _resident((1, D_MODEL)),
        ],
        out_specs=pl.BlockSpec((tm, D_MODEL), lambda i: (i, 0)),
        out_shape=jax.ShapeDtypeStruct((N, D_MODEL), F32),
        scratch_shapes=[pltpu.VMEM((tm, D_MODEL), F32)],
        compiler_params=_params("parallel"),
        name="outproj_ffn",
    )(x, o, wo, g, win, wout, gout)


def _ret_inproj_kernel(x_ref, g_ref, wq_ref, wkt_ref, wv_ref, wg_ref, c_ref, a_ref, b_ref,
                       ct_ref, at_ref, bt_ref, q_ref, kt_ref, v_ref, sg_ref):
    hn = (_rms(x_ref[...]) * g_ref[...]).astype(BF16)
    q = jnp.dot(hn, wq_ref[...], preferred_element_type=F32)
    c, a, b = c_ref[...], a_ref[...], b_ref[...]
    for h in range(RET_HEADS):
        sl = slice(h * RET_DK, (h + 1) * RET_DK)
        qh = q[:, sl]
        qr = qh * c + pltpu.roll(qh, RET_DK - 1, 1) * a + pltpu.roll(qh, 1, 1) * b
        q_ref[:, sl] = qr.astype(BF16)
    kt = lax.dot_general(wkt_ref[...], hn, _NT, preferred_element_type=F32)
    ct, at, bt = ct_ref[...], at_ref[...], bt_ref[...]
    for h in range(RET_HEADS):
        sl = slice(h * RET_DK, (h + 1) * RET_DK)
        kh = kt[sl]
        kr = kh * ct + pltpu.roll(kh, RET_DK - 1, 0) * at + pltpu.roll(kh, 1, 0) * bt
        kt_ref[sl, :] = kr.astype(BF16)
    v_ref[...] = jnp.dot(hn, wv_ref[...], preferred_element_type=F32).astype(BF16)
    gate = jnp.dot(hn, wg_ref[...], preferred_element_type=F32)
    sg_ref[...] = (gate * jax.nn.sigmoid(gate)).astype(BF16)


def _ret_inproj(x, g, wq, wkt, wv, wg, tabs, tm):
    B, F, _ = x.shape
    c, a, b, ct, at, bt = tabs
    tok = lambda bb, i: (bb, i, 0)
    tab = pl.BlockSpec((tm, RET_DK), lambda bb, i: (i, 0))
    tab_t = pl.BlockSpec((RET_DK, tm), lambda bb, i: (0, i))
    return pl.pallas_call(
        _ret_inproj_kernel,
        grid=(B, F // tm),
        in_specs=[
            pl.BlockSpec((None, tm, D_MODEL), tok),
            _resident((1, D_MODEL)),
            _resident((D_MODEL, RET_QK)),
            _resident((RET_QK, D_MODEL)),
            _resident((D_MODEL, RET_VD)),
            _resident((D_MODEL, RET_VD)),
            tab, tab, tab, tab_t, tab_t, tab_t,
        ],
        out_specs=[
            pl.BlockSpec((None, tm, RET_QK), tok),
            pl.BlockSpec((None, RET_QK, tm), lambda bb, i: (bb, 0, i)),
            pl.BlockSpec((None, tm, RET_VD), tok),
            pl.BlockSpec((None, tm, RET_VD), tok),
        ],
        out_shape=[
            jax.ShapeDtypeStruct((B, F, RET_QK), BF16),
            jax.ShapeDtypeStruct((B, RET_QK, F), BF16),
            jax.ShapeDtypeStruct((B, F, RET_VD), BF16),
            jax.ShapeDtypeStruct((B, F, RET_VD), BF16),
        ],
        compiler_params=_params("parallel", "parallel"),
        name="ret_inproj",
    )(x, g, wq, wkt, wv, wg, c, a, b, ct, at, bt)


def _ret_rot_tables(pos):
    angle = 1.0 / (RET_THETA ** jnp.linspace(0.0, 1.0, RET_DK // 2, dtype=F32))
    angle = jnp.repeat(angle, 2)
    ang = pos.astype(F32)[:, None] * angle[None, :]
    cos, sin = jnp.cos(ang), jnp.sin(ang)
    even = (jnp.arange(RET_DK) % 2 == 0)[None, :]
    a = jnp.where(even, -sin, 0.0)
    b = jnp.where(even, 0.0, sin)
    ks = RET_DK ** -0.5
    return cos, a, b, (cos * ks).T, (a * ks).T, (b * ks).T


def _ret_log_decay(h):
    return math.log(1.0 - 2.0 ** (-5.0 - h))


def _ret_gate_out(o, sg):
    return (sg.astype(F32) * _rms(o)).astype(BF16)


def _ret_prompt_kernel(q_ref, kt_ref, v_ref, sg_ref, og_ref, st_ref, s_sc, *, n_valid):
    c_idx = pl.program_id(1)

    @pl.when(c_idx == 0)
    def _():
        s_sc[...] = jnp.zeros(s_sc.shape, F32)

    C = RET_CHUNK
    valid = jnp.clip(n_valid - c_idx * C, 0, C).astype(F32)
    valid_v = jnp.full((1, 1), valid, F32)
    n_row = lax.broadcasted_iota(jnp.int32, (C, 1), 0).astype(F32)
    n_col = lax.broadcasted_iota(jnp.int32, (1, C), 1).astype(F32)
    diff = n_row - n_col
    for h in range(RET_HEADS):
        lg = _ret_log_decay(h)
        q = q_ref[:, h * RET_DK:(h + 1) * RET_DK]
        kt = kt_ref[h * RET_DK:(h + 1) * RET_DK, :]
        v = v_ref[:, h * RET_DV:(h + 1) * RET_DV]
        dmask = jnp.where(diff >= 0, jnp.exp(lg * jnp.maximum(diff, 0.0)), 0.0)
        s = jnp.dot(q, kt, preferred_element_type=F32) * dmask
        inner = jnp.dot(s.astype(BF16), v, preferred_element_type=F32)
        state = s_sc[h]
        q_dec = (q.astype(F32) * jnp.exp(lg * (n_row + 1.0))).astype(BF16)
        cross = jnp.dot(q_dec, state.astype(BF16), preferred_element_type=F32)
        k_decay = jnp.where(n_col < valid, jnp.exp(lg * (valid - 1.0 - n_col)), 0.0)
        k_dec = (kt.astype(F32) * k_decay).astype(BF16)
        s_sc[h] = jnp.exp(lg * valid_v) * state + jnp.dot(k_dec, v, preferred_element_type=F32)
        sl = slice(h * RET_DV, (h + 1) * RET_DV)
        og_ref[:, sl] = _ret_gate_out(inner + cross, sg_ref[:, sl])

    @pl.when(c_idx == pl.num_programs(1) - 1)
    def _():
        st_ref[...] = s_sc[...]


def _ret_prompt(q, kt, v, sg, n_valid):
    B, F, _ = q.shape
    C = RET_CHUNK
    tok = lambda b, c: (b, c, 0)
    return pl.pallas_call(
        functools.partial(_ret_prompt_kernel, n_valid=n_valid),
        grid=(B, F // C),
        in_specs=[
            pl.BlockSpec((None, C, RET_QK), tok),
            pl.BlockSpec((None, RET_QK, C), lambda b, c: (b, 0, c)),
            pl.BlockSpec((None, C, RET_VD), tok),
            pl.BlockSpec((None, C, RET_VD), tok),
        ],
        out_specs=[
            pl.BlockSpec((None, C, RET_VD), tok),
            pl.BlockSpec((None, RET_HEADS, RET_DK, RET_DV), lambda b, c: (b, 0, 0, 0)),
        ],
        out_shape=[
            jax.ShapeDtypeStruct((B, F, RET_VD), BF16),
            jax.ShapeDtypeStruct((B, RET_HEADS, RET_DK, RET_DV), F32),
        ],
        scratch_shapes=[pltpu.VMEM((RET_HEADS, RET_DK, RET_DV), F32)],
        compiler_params=_params("parallel", "arbitrary"),
        name="ret_prompt",
    )(q, kt, v, sg)


def _ret_sample_kernel(q_ref, k_ref, v_ref, sg_ref, st_in_ref, og_ref, st_out_ref, *, n_new):
    T = n_new
    n_row = lax.broadcasted_iota(jnp.int32, (T, 1), 0).astype(F32)
    n_col = lax.broadcasted_iota(jnp.int32, (1, T), 1).astype(F32)
    diff = n_row - n_col
    for h in range(RET_HEADS):
        lg = _ret_log_decay(h)
        q = q_ref[:, h * RET_DK:(h + 1) * RET_DK].astype(F32)
        k = k_ref[:, h * RET_DK:(h + 1) * RET_DK].astype(F32)
        v = v_ref[:, h * RET_DV:(h + 1) * RET_DV].astype(F32)
        dmask = jnp.where(diff >= 0, jnp.exp(lg * jnp.maximum(diff, 0.0)), 0.0)
        s = lax.dot_general(q, k, _NT, preferred_element_type=F32) * dmask
        inner = jnp.dot(s, v, preferred_element_type=F32)
        state = st_in_ref[h]
        q_dec = (q * jnp.exp(lg * (n_row + 1.0))).astype(BF16)
        cross = jnp.dot(q_dec, state.astype(BF16), preferred_element_type=F32)
        k_dec = k * jnp.exp(lg * (T - 1.0 - n_row))
        st_out_ref[h] = math.exp(lg * T) * state + lax.dot_general(k_dec, v, _TN, preferred_element_type=F32)
        sl = slice(h * RET_DV, (h + 1) * RET_DV)
        og_ref[:, sl] = _ret_gate_out(inner + cross, sg_ref[:, sl])


def _ret_sample(q, k, v, sg, state_all, layer, prev):
    DB, T, _ = q.shape
    tok = lambda b: (b, 0, 0)
    st = pl.BlockSpec((None, None, RET_HEADS, RET_DK, RET_DV), lambda b: (layer, b, 0, 0, 0))
    n_in = 5
    in_specs = [
        pl.BlockSpec((None, T, RET_QK), tok),
        pl.BlockSpec((None, T, RET_QK), tok),
        pl.BlockSpec((None, T, RET_VD), tok),
        pl.BlockSpec((None, T, RET_VD), tok),
        st,
    ]
    body = functools.partial(_ret_sample_kernel, n_new=T)
    aliases, extra = {}, ()
    if prev is not None:
        body = _drop_refs(body, n_in, 1)
        in_specs.append(pl.BlockSpec(memory_space=pl.ANY))
        aliases = {n_in: 1}
        extra = (prev,)
    return pl.pallas_call(
        body,
        grid=(DB,),
        in_specs=in_specs,
        out_specs=[pl.BlockSpec((None, T, RET_VD), tok), st],
        out_shape=[
            jax.ShapeDtypeStruct((DB, T, RET_VD), BF16),
            jax.ShapeDtypeStruct(state_all.shape, F32),
        ],
        input_output_aliases=aliases,
        compiler_params=_params("parallel"),
        name="ret_sample",
    )(q, k, v, sg, state_all, *extra)


def kernel(x_prompt, x_sample, cache_k, cache_v, state_ret, page_table, meta_tokens, norm_mix, norm_ffn, norm_out, da_w_in, da_w_out, da_lambda_q1, da_lambda_k1, da_lambda_q2, da_lambda_k2, da_subln, ret_w_in, ret_w_out, ffn_w_in, ffn_w_out):
    B, seq, _ = x_prompt.shape
    DB, T, _ = x_sample.shape
    depth = norm_mix.shape[0]
    L = N_META + seq
    NS = DB * T
    page = cache_k.shape[2]
    past = page_table.shape[1] * page

    tp = _tiles(L)
    ts = {k: min(t, NS) for k, t in _tiles(NS).items()}
    F = _round_up(L, tp["frame"])

    meta = jnp.broadcast_to(meta_tokens.astype(F32)[None], (B, N_META, D_MODEL))
    hp = jnp.concatenate([meta, x_prompt, jnp.zeros((B, F - L, D_MODEL), F32)], axis=1)
    hs = x_sample.reshape(1, NS, D_MODEL)

    pos_p = jnp.arange(F)
    pos_s = past + jnp.arange(NS) % T
    da_tabs_p, da_tabs_s = _da_rope_tables(pos_p), _da_rope_tables(pos_s)
    ret_tabs_p, ret_tabs_s = _ret_rot_tables(pos_p), _ret_rot_tables(pos_s)

    kc = jnp.transpose(cache_k, (0, 1, 3, 4, 5, 2)).reshape(cache_k.shape[0], cache_k.shape[1], DA_HEADS, 2 * DA_DK, page)
    vc = cache_v.reshape(cache_v.shape[0], cache_v.shape[1], page * DA_HEADS, DA_DV)

    row = lambda a: a.astype(F32).reshape(1, -1)
    st_s_all = None
    kp_rows, vp_rows, ks_rows, vs_rows, st_p_list = [], [], [], [], []
    for i in range(depth):
        j = i // N_MIXERS
        g_mix = row(norm_mix[i])
        if i % N_MIXERS == 0:
            lam_init = 0.8 - 0.6 * math.exp(-0.3 * i)
            w = da_w_in[j]
            wq = w[:, :D_MODEL].astype(BF16)
            wkt = w[:, D_MODEL:2 * D_MODEL].T.astype(BF16)
            wv = w[:, 2 * D_MODEL:].astype(BF16)
            lams = (row(da_lambda_q1[j]), row(da_lambda_k1[j]), row(da_lambda_q2[j]), row(da_lambda_k2[j]))
            g_sub = row(da_subln[j])
            qd, ktf, ktb, vf, vb = _da_inproj(hp, g_mix, wq, wkt, wv, da_tabs_p, L, tp["tm_proj"])
            qd_s, ktf_s, _, vf_s, _ = _da_inproj(hs, g_mix, wq, wkt, wv, da_tabs_s, NS, ts["tm_proj"])
            op = _da_prompt(qd, ktb, vb, lams, g_sub, lam_init, tp["tq"], tp["tk"])
            q2 = qd_s.reshape(2, DB, T, D_MODEL).transpose(1, 0, 2, 3).reshape(DB, 2 * T, D_MODEL)
            kn = ktf_s[0].T.reshape(DB, T, D_MODEL)
            vn = vf_s.reshape(DB, T, DA_HEADS, DA_DV)
            osm = _da_sample(page_table, j, q2, kc, vc, kn, vn, lams, g_sub, lam_init)
            o_p, o_s = op.reshape(B * F, D_MODEL), osm.reshape(NS, D_MODEL)
            wo = da_w_out[j].astype(BF16)
            kp_rows.append(ktf.reshape(B, DA_HEADS, 2, DA_DK, L).transpose(0, 4, 1, 2, 3))
            vp_rows.append(vf)
            ks_rows.append(ktf_s.reshape(DA_HEADS, 2, DA_DK, DB, T).transpose(3, 4, 0, 1, 2))
            vs_rows.append(vn)
        else:
            w = ret_w_in[j]
            wq = w[:, :RET_QK].astype(BF16)
            wkt = w[:, RET_QK:2 * RET_QK].T.astype(BF16)
            wv = w[:, 2 * RET_QK:2 * RET_QK + RET_VD].astype(BF16)
            wg = w[:, 2 * RET_QK + RET_VD:].astype(BF16)
            q, kt, v, sg = _ret_inproj(hp, g_mix, wq, wkt, wv, wg, ret_tabs_p, tp["tm_proj"])
            q_s, kt_s, v_s, sg_s = _ret_inproj(hs, g_mix, wq, wkt, wv, wg, ret_tabs_s, ts["tm_proj"])
            og, st_p = _ret_prompt(q, kt, v, sg, L)
            per_seq = lambda a: a.reshape(DB, T, a.shape[-1])
            og_s, st_s_all = _ret_sample(per_seq(q_s[0]), per_seq(kt_s[0].T), per_seq(v_s[0]), per_seq(sg_s[0]),
                                         state_ret, j, st_s_all)
            o_p, o_s = og.reshape(B * F, RET_VD), og_s.reshape(NS, RET_VD)
            wo = ret_w_out[j].astype(BF16)
            st_p_list.append(st_p)
        last = i == depth - 1
        ffn = (wo, row(norm_ffn[i]), ffn_w_in[i].astype(BF16), ffn_w_out[i].astype(BF16), row(norm_out))
        hp = _outproj_ffn(hp.reshape(B * F, D_MODEL), o_p, *ffn, tp["tm_ffn"], last).reshape(B, F, D_MODEL)
        hs = _outproj_ffn(hs.reshape(NS, D_MODEL), o_s, *ffn, ts["tm_ffn"], last).reshape(1, NS, D_MODEL)
    y_prompt = hp[:, N_META:L]
    y_sample = hs.reshape(DB, T, D_MODEL)
    return (y_prompt, y_sample, jnp.stack(kp_rows), jnp.stack(vp_rows), jnp.stack(ks_rows),
            jnp.stack(vs_rows), jnp.stack(st_p_list), st_s_all)
```

```python
import functools
import math

import jax
import jax.numpy as jnp
from jax import lax
from jax.experimental import pallas as pl
from jax.experimental.pallas import tpu as pltpu

F32, BF16 = jnp.float32, jnp.bfloat16

D_MODEL = 1024
N_META = 16
N_MIXERS = 2
DA_HEADS = 8
DA_DK = 64
DA_DV = 128
ROPE_THETA = 500000.0
ROPE_DIMS = 16
RET_HEADS = 4
RET_DK = 256
RET_DV = 512
RET_QK = RET_HEADS * RET_DK
RET_VD = RET_HEADS * RET_DV
RET_CHUNK = 256
RET_THETA = 10000.0
FFN_HIDDEN = 2816
FFN_CHUNK = 256
RMS_EPS = 1e-6
MASK_VALUE = -1e30
LANES = 128
VMEM_LIMIT = 56 * 1024 * 1024

_NT = (((1,), (1,)), ((), ()))
_TN = (((0,), (0,)), ((), ()))


def _tiles(n_rows):
    if n_rows >= 4096:
        return dict(frame=768, tm_proj=384, tm_ffn=768, tq=768, tk=768, td=256)
    if n_rows >= 512:
        return dict(frame=512, tm_proj=512, tm_ffn=512, tq=512, tk=256, td=256)
    return dict(frame=256, tm_proj=128, tm_ffn=128, tq=256, tk=128, td=128)


def _round_up(x, m):
    return -(-x // m) * m


def _rms(x):
    return x * lax.rsqrt(jnp.mean(x * x, axis=-1, keepdims=True) + RMS_EPS)


def _resident(shape):
    return pl.BlockSpec(shape, lambda *_: (0,) * len(shape), pipeline_mode=pl.Buffered(1))


def _params(*sem):
    return pltpu.CompilerParams(dimension_semantics=sem, vmem_limit_bytes=VMEM_LIMIT)


def _da_inproj_kernel(x_ref, g_ref, wq_ref, wkt_ref, wv_ref, c_ref, s1_ref, s2_ref, ct_ref, st_ref,
                      qd_ref, ktf_ref, ktb_ref, vf_ref, vb_ref):
    hn = (_rms(x_ref[...]) * g_ref[...]).astype(BF16)
    lane = lax.broadcasted_iota(jnp.int32, (1, LANES), 1)
    comp0 = (lane < DA_DK).astype(F32)
    comp1 = 1.0 - comp0
    q = jnp.dot(hn, wq_ref[...], preferred_element_type=F32)
    c, s1, s2 = c_ref[...], s1_ref[...], s2_ref[...]
    for h in range(DA_HEADS):
        sl = slice(h * LANES, (h + 1) * LANES)
        qh = q[:, sl]
        qr = qh * c + pltpu.roll(qh, ROPE_DIMS // 2, 1) * s1 + pltpu.roll(qh, LANES - ROPE_DIMS // 2, 1) * s2
        qd_ref[0, :, sl] = (qr * comp0).astype(BF16)
        qd_ref[1, :, sl] = (qr * comp1).astype(BF16)
    kt = lax.dot_general(wkt_ref[...], hn, _NT, preferred_element_type=F32)
    ct, st = ct_ref[...], st_ref[...]
    half = ROPE_DIMS // 2
    for grp in range(2 * DA_HEADS):
        b0 = grp * DA_DK
        x1 = kt[b0:b0 + half]
        x2 = kt[b0 + half:b0 + 2 * half]
        rest = kt[b0 + 2 * half:b0 + DA_DK]
        o1 = x1 * ct - x2 * st
        o2 = x2 * ct + x1 * st
        for lo, val in ((b0, o1), (b0 + half, o2), (b0 + 2 * half, rest)):
            ktf_ref[lo:lo + val.shape[0], :] = val
            ktb_ref[lo:lo + val.shape[0], :] = val.astype(BF16)
    v = jnp.dot(hn, wv_ref[...], preferred_element_type=F32)
    vb_ref[...] = v.astype(BF16)
    for h in range(DA_HEADS):
        vf_ref[:, h, :] = v[:, h * DA_DV:(h + 1) * DA_DV]


def _drop_refs(body, start, count):
    def wrapped(*refs):
        return body(*refs[:start], *refs[start + count:])
    return wrapped


def _da_inproj(x, g, wq, wkt, wv, tabs, n_valid, tm):
    B, F, _ = x.shape
    c, s1, s2, ct, st = tabs
    tok = lambda b, i: (b, i, 0)
    return pl.pallas_call(
        _da_inproj_kernel,
        grid=(B, F // tm),
        in_specs=[
            pl.BlockSpec((None, tm, D_MODEL), tok),
            _resident((1, D_MODEL)),
            _resident((D_MODEL, D_MODEL)),
            _resident((D_MODEL, D_MODEL)),
            _resident((D_MODEL, D_MODEL)),
            pl.BlockSpec((tm, LANES), lambda b, i: (i, 0)),
            pl.BlockSpec((tm, LANES), lambda b, i: (i, 0)),
            pl.BlockSpec((tm, LANES), lambda b, i: (i, 0)),
            pl.BlockSpec((ROPE_DIMS // 2, tm), lambda b, i: (0, i)),
            pl.BlockSpec((ROPE_DIMS // 2, tm), lambda b, i: (0, i)),
        ],
        out_specs=[
            pl.BlockSpec((2, None, tm, D_MODEL), lambda b, i: (0, b, i, 0)),
            pl.BlockSpec((None, D_MODEL, tm), lambda b, i: (b, 0, i)),
            pl.BlockSpec((None, D_MODEL, tm), lambda b, i: (b, 0, i)),
            pl.BlockSpec((None, tm, DA_HEADS, DA_DV), lambda b, i: (b, i, 0, 0)),
            pl.BlockSpec((None, tm, D_MODEL), tok),
        ],
        out_shape=[
            jax.ShapeDtypeStruct((2, B, F, D_MODEL), BF16),
            jax.ShapeDtypeStruct((B, D_MODEL, n_valid), F32),
            jax.ShapeDtypeStruct((B, D_MODEL, F), BF16),
            jax.ShapeDtypeStruct((B, n_valid, DA_HEADS, DA_DV), F32),
            jax.ShapeDtypeStruct((B, F, D_MODEL), BF16),
        ],
        compiler_params=_params("parallel", "parallel"),
        name="da_inproj",
    )(x, g, wq, wkt, wv, c, s1, s2, ct, st)


def _da_rope_tables(pos):
    half = ROPE_DIMS // 2
    inv = ROPE_THETA ** (-jnp.arange(0, ROPE_DIMS, 2, dtype=F32) / ROPE_DIMS)
    ang = pos.astype(F32)[:, None] * inv[None, :]
    cos, sin = jnp.cos(ang), jnp.sin(ang)
    n = pos.shape[0]
    zeros = lambda w: jnp.zeros((n, w), F32)
    c64 = jnp.concatenate([cos, cos, jnp.ones((n, DA_DK - ROPE_DIMS), F32)], axis=1)
    s1_64 = jnp.concatenate([zeros(half), sin, zeros(DA_DK - ROPE_DIMS)], axis=1)
    s2_64 = jnp.concatenate([-sin, zeros(DA_DK - half)], axis=1)
    scale = DA_DK ** -0.5 * math.log2(math.e)
    tile2 = lambda a: jnp.concatenate([a, a], axis=1) * scale
    return tile2(c64), tile2(s1_64), tile2(s2_64), cos.T, sin.T


def _da_lambda(lq1_ref, lk1_ref, lq2_ref, lk2_ref, lam_init):
    a = jnp.sum(lq1_ref[...] * lk1_ref[...], axis=-1, keepdims=True)
    b = jnp.sum(lq2_ref[...] * lk2_ref[...], axis=-1, keepdims=True)
    return jnp.exp(a) - jnp.exp(b) + lam_init


def _da_finish(acc0, l0, acc1, l1, lam, g, lam_init):
    o = acc0 / l0 - lam * (acc1 / l1)
    return _rms(o) * g * (1.0 - lam_init)


def _da_prompt_kernel(q_ref, kt_ref, v_ref, lq1_ref, lk1_ref, lq2_ref, lk2_ref, g_ref, o_ref,
                      m_sc, l_sc, acc_sc, *, tq, tk, td, lam_init):
    i = pl.program_id(2)
    m_sc[...] = jnp.full(m_sc.shape, MASK_VALUE, F32)
    l_sc[...] = jnp.zeros(l_sc.shape, F32)
    acc_sc[...] = jnp.zeros(acc_sc.shape, F32)

    def step(off, r0, width, masked):
        kt = kt_ref[:, pl.ds(off, width)]
        v = v_ref[pl.ds(off, width), :]
        for c in range(2):
            s = jnp.dot(q_ref[c, r0:, :], kt, preferred_element_type=F32)
            if masked:
                qpos = i * tq + r0 + lax.broadcasted_iota(jnp.int32, s.shape, 0)
                kpos = off + lax.broadcasted_iota(jnp.int32, s.shape, 1)
                s = jnp.where(kpos <= qpos, s, MASK_VALUE)
            tiles = [s[:, t * LANES:(t + 1) * LANES] for t in range(width // LANES)]
            m_old = m_sc[c, r0:, :]
            blk_max = jnp.max(functools.reduce(jnp.maximum, tiles), axis=-1, keepdims=True)
            m_new = jnp.maximum(m_old, blk_max)
            alpha = jnp.exp2(m_old - m_new)
            ps = [jnp.exp2(t - m_new) for t in tiles]
            l_sc[c, r0:, :] = alpha * l_sc[c, r0:, :] + functools.reduce(jnp.add, ps)
            p = jnp.concatenate(ps, axis=1).astype(BF16)
            acc_sc[c, r0:, :] = alpha * acc_sc[c, r0:, :] + jnp.dot(p, v, preferred_element_type=F32)
            m_sc[c, r0:, :] = m_new

    def full(j):
        step(pl.multiple_of(j * tk, tk), 0, tk, False)

    n_full = i * (tq // tk)

    def pair_body(j, carry):
        step(pl.multiple_of(2 * j * tk, tk), 0, 2 * tk, False)
        return carry

    lax.fori_loop(0, n_full // 2, pair_body, 0)

    @pl.when(n_full % 2 == 1)
    def _():
        full(n_full - 1)

    for d in range(tq // td):
        step(pl.multiple_of(i * tq + d * td, td), d * td, td, True)

    lam = _da_lambda(lq1_ref, lk1_ref, lq2_ref, lk2_ref, lam_init)
    l0 = jnp.sum(l_sc[0], axis=-1, keepdims=True)
    l1 = jnp.sum(l_sc[1], axis=-1, keepdims=True)
    o = _da_finish(acc_sc[0], l0, acc_sc[1], l1, lam, g_ref[...], lam_init)
    o_ref[...] = o.astype(o_ref.dtype)


def _da_prompt(qd, ktb, vb, lams, g, lam_init, tq, tk, td):
    _, B, F, _ = qd.shape
    lam_spec = pl.BlockSpec((1, DA_DK), lambda b, h, i: (0, 0))
    return pl.pallas_call(
        functools.partial(_da_prompt_kernel, tq=tq, tk=tk, td=td, lam_init=lam_init),
        grid=(B, DA_HEADS, F // tq),
        in_specs=[
            pl.BlockSpec((2, None, tq, LANES), lambda b, h, i: (0, b, i, h)),
            pl.BlockSpec((None, LANES, F), lambda b, h, i: (b, h, 0)),
            pl.BlockSpec((None, F, LANES), lambda b, h, i: (b, 0, h)),
            lam_spec, lam_spec, lam_spec, lam_spec,
            pl.BlockSpec((1, DA_DV), lambda b, h, i: (0, 0)),
        ],
        out_specs=pl.BlockSpec((None, tq, LANES), lambda b, h, i: (b, i, h)),
        out_shape=jax.ShapeDtypeStruct((B, F, D_MODEL), BF16),
        scratch_shapes=[
            pltpu.VMEM((2, tq, LANES), F32),
            pltpu.VMEM((2, tq, LANES), F32),
            pltpu.VMEM((2, tq, LANES), F32),
        ],
        compiler_params=_params("parallel", "parallel", "parallel"),
        name="da_prompt_attn",
    )(qd, ktb, vb, *lams, g)


def _da_sample_kernel(pt_ref, q_ref, kn_ref, vn_ref, lq1_ref, lk1_ref, lq2_ref, lk2_ref, g_ref, *rest,
                      n_pages, n_new, lam_init):
    del pt_ref
    kc_refs, vc_refs, o_ref = rest[:n_pages], rest[n_pages:2 * n_pages], rest[2 * n_pages]
    page = vc_refs[0].shape[0] // DA_HEADS
    lam = _da_lambda(lq1_ref, lk1_ref, lq2_ref, lk2_ref, lam_init)
    q = q_ref[...]
    row = lax.broadcasted_iota(jnp.int32, (2 * n_new, 1), 0)
    step = jnp.where(row >= n_new, row - n_new, row)
    for h in range(DA_HEADS):
        sl = slice(h * LANES, (h + 1) * LANES)
        qh = q[:, sl]
        kt = jnp.concatenate([r[h].astype(BF16) for r in kc_refs], axis=1)
        s = jnp.dot(qh, kt, preferred_element_type=F32)
        tiles = [s[:, t * LANES:(t + 1) * LANES] for t in range(s.shape[1] // LANES)]
        qf = qh.astype(F32)
        kn = kn_ref[:, sl]
        s_new = [jnp.where(step >= j, jnp.sum(qf * kn[j:j + 1, :], axis=-1, keepdims=True), MASK_VALUE)
                 for j in range(n_new)]
        m = jnp.max(functools.reduce(jnp.maximum, tiles), axis=-1, keepdims=True)
        m = functools.reduce(jnp.maximum, s_new, m)
        ps = [jnp.exp2(t - m) for t in tiles]
        l = jnp.sum(functools.reduce(jnp.add, ps), axis=-1, keepdims=True)
        v = jnp.concatenate([r[pl.ds(h, page, stride=DA_HEADS), :].astype(BF16) for r in vc_refs], axis=0)
        acc = jnp.dot(jnp.concatenate(ps, axis=1).astype(BF16), v, preferred_element_type=F32)
        for j in range(n_new):
            pj = jnp.exp2(s_new[j] - m)
            l = l + pj
            acc = acc + pj * vn_ref[j:j + 1, h, :]
        o_ref[:, sl] = _da_finish(acc[:n_new], l[:n_new], acc[n_new:], l[n_new:], lam, g_ref[...], lam_init)


def _da_sample(page_table, layer, q2, kc, vc, kn, vn, lams, g, lam_init):
    DB, n_pages = page_table.shape
    T = kn.shape[1]
    page = kc.shape[-1]
    lam_spec = pl.BlockSpec((1, DA_DK), lambda b, pt: (0, 0))
    k_spec = lambda r: pl.BlockSpec((None, None, DA_HEADS, LANES, page), lambda b, pt: (layer, pt[b, r], 0, 0, 0))
    v_spec = lambda r: pl.BlockSpec((None, None, page * DA_HEADS, DA_DV), lambda b, pt: (layer, pt[b, r], 0, 0))
    grid_spec = pltpu.PrefetchScalarGridSpec(
        num_scalar_prefetch=1,
        grid=(DB,),
        in_specs=[
            pl.BlockSpec((None, 2 * T, D_MODEL), lambda b, pt: (b, 0, 0)),
            pl.BlockSpec((None, T, D_MODEL), lambda b, pt: (b, 0, 0)),
            pl.BlockSpec((None, T, DA_HEADS, DA_DV), lambda b, pt: (b, 0, 0, 0)),
            lam_spec, lam_spec, lam_spec, lam_spec,
            pl.BlockSpec((1, DA_DV), lambda b, pt: (0, 0)),
        ] + [k_spec(r) for r in range(n_pages)] + [v_spec(r) for r in range(n_pages)],
        out_specs=pl.BlockSpec((None, T, D_MODEL), lambda b, pt: (b, 0, 0)),
    )
    return pl.pallas_call(
        functools.partial(_da_sample_kernel, n_pages=n_pages, n_new=T, lam_init=lam_init),
        grid_spec=grid_spec,
        out_shape=jax.ShapeDtypeStruct((DB, T, D_MODEL), F32),
        compiler_params=_params("parallel"),
        name="da_sample_attn",
    )(page_table, q2, kn, vn, *lams, g, *([kc] * n_pages), *([vc] * n_pages))


def _ffn_kernel(x_ref, o_ref, wo_ref, g_ref, win_ref, wout_ref, gout_ref, y_ref, acc_sc, *, final_norm):
    h = x_ref[...] + jnp.dot(o_ref[...].astype(BF16), wo_ref[...], preferred_element_type=F32)
    hn = (_rms(h) * g_ref[...]).astype(BF16)
    acc_sc[...] = h
    for j in range(FFN_HIDDEN // FFN_CHUNK):
        lo = j * FFN_CHUNK
        gate = jnp.dot(hn, win_ref[:, lo:lo + FFN_CHUNK], preferred_element_type=F32)
        up = jnp.dot(hn, win_ref[:, FFN_HIDDEN + lo:FFN_HIDDEN + lo + FFN_CHUNK], preferred_element_type=F32)
        act = (gate * jax.nn.sigmoid(gate) * up).astype(BF16)
        acc_sc[...] += jnp.dot(act, wout_ref[lo:lo + FFN_CHUNK, :], preferred_element_type=F32)
    y = acc_sc[...]
    if final_norm:
        y = _rms(y) * gout_ref[...]
    y_ref[...] = y


def _outproj_ffn(x, o, wo, g, win, wout, gout, tm, final_norm):
    N = x.shape[0]
    ko = o.shape[1]
    return pl.pallas_call(
        functools.partial(_ffn_kernel, final_norm=final_norm),
        grid=(N // tm,),
        in_specs=[
            pl.BlockSpec((tm, D_MODEL), lambda i: (i, 0)),
            pl.BlockSpec((tm, ko), lambda i: (i, 0)),
            _resident((ko, D_MODEL)),
            _resident((1, D_MODEL)),
            _resident((D_MODEL, 2 * FFN_HIDDEN)),
            _resident((FFN_HIDDEN, D_MODEL)),
            _resident((1, D_MODEL)),
        ],
        out_specs=pl.BlockSpec((tm, D_MODEL), lambda i: (i, 0)),
        out_shape=jax.ShapeDtypeStruct((N, D_MODEL), F32),
        scratch_shapes=[pltpu.VMEM((tm, D_MODEL), F32)],
        compiler_params=_params("parallel"),
        name="outproj_ffn",
    )(x, o, wo, g, win, wout, gout)


def _ret_inproj_kernel(x_ref, g_ref, wq_ref, wkt_ref, wv_ref, wg_ref, c_ref, a_ref, b_ref,
                       ct_ref, at_ref, bt_ref, q_ref, kt_ref, v_ref, sg_ref):
    hn = (_rms(x_ref[...]) * g_ref[...]).astype(BF16)
    q = jnp.dot(hn, wq_ref[...], preferred_element_type=F32)
    c, a, b = c_ref[...], a_ref[...], b_ref[...]
    for h in range(RET_HEADS):
        sl = slice(h * RET_DK, (h + 1) * RET_DK)
        qh = q[:, sl]
        qr = qh * c + pltpu.roll(qh, RET_DK - 1, 1) * a + pltpu.roll(qh, 1, 1) * b
        q_ref[:, sl] = qr.astype(BF16)
    kt = lax.dot_general(wkt_ref[...], hn, _NT, preferred_element_type=F32)
    ct, at, bt = ct_ref[...], at_ref[...], bt_ref[...]
    for h in range(RET_HEADS):
        sl = slice(h * RET_DK, (h + 1) * RET_DK)
        kh = kt[sl]
        kr = kh * ct + pltpu.roll(kh, RET_DK - 1, 0) * at + pltpu.roll(kh, 1, 0) * bt
        kt_ref[sl, :] = kr.astype(BF16)
    v_ref[...] = jnp.dot(hn, wv_ref[...], preferred_element_type=F32).astype(BF16)
    gate = jnp.dot(hn, wg_ref[...], preferred_element_type=F32)
    sg_ref[...] = (gate * jax.nn.sigmoid(gate)).astype(BF16)


def _ret_inproj(x, g, wq, wkt, wv, wg, tabs, tm):
    B, F, _ = x.shape
    c, a, b, ct, at, bt = tabs
    tok = lambda bb, i: (bb, i, 0)
    tab = pl.BlockSpec((tm, RET_DK), lambda bb, i: (i, 0))
    tab_t = pl.BlockSpec((RET_DK, tm), lambda bb, i: (0, i))
    return pl.pallas_call(
        _ret_inproj_kernel,
        grid=(B, F // tm),
        in_specs=[
            pl.BlockSpec((None, tm, D_MODEL), tok),
            _resident((1, D_MODEL)),
            _resident((D_MODEL, RET_QK)),
            _resident((RET_QK, D_MODEL)),
            _resident((D_MODEL, RET_VD)),
            _resident((D_MODEL, RET_VD)),
            tab, tab, tab, tab_t, tab_t, tab_t,
        ],
        out_specs=[
            pl.BlockSpec((None, tm, RET_QK), tok),
            pl.BlockSpec((None, RET_QK, tm), lambda bb, i: (bb, 0, i)),
            pl.BlockSpec((None, tm, RET_VD), tok),
            pl.BlockSpec((None, tm, RET_VD), tok),
        ],
        out_shape=[
            jax.ShapeDtypeStruct((B, F, RET_QK), BF16),
            jax.ShapeDtypeStruct((B, RET_QK, F), BF16),
            jax.ShapeDtypeStruct((B, F, RET_VD), BF16),
            jax.ShapeDtypeStruct((B, F, RET_VD), BF16),
        ],
        compiler_params=_params("parallel", "parallel"),
        name="ret_inproj",
    )(x, g, wq, wkt, wv, wg, c, a, b, ct, at, bt)


def _ret_rot_tables(pos):
    angle = 1.0 / (RET_THETA ** jnp.linspace(0.0, 1.0, RET_DK // 2, dtype=F32))
    angle = jnp.repeat(angle, 2)
    ang = pos.astype(F32)[:, None] * angle[None, :]
    cos, sin = jnp.cos(ang), jnp.sin(ang)
    even = (jnp.arange(RET_DK) % 2 == 0)[None, :]
    a = jnp.where(even, -sin, 0.0)
    b = jnp.where(even, 0.0, sin)
    ks = RET_DK ** -0.5
    return cos, a, b, (cos * ks).T, (a * ks).T, (b * ks).T


def _ret_log_decay(h):
    return math.log(1.0 - 2.0 ** (-5.0 - h))


def _ret_gate_out(o, sg):
    return (sg.astype(F32) * _rms(o)).astype(BF16)


def _ret_prompt_kernel(q_ref, kt_ref, v_ref, sg_ref, og_ref, st_ref, s_sc, *, n_valid):
    c_idx = pl.program_id(1)

    @pl.when(c_idx == 0)
    def _():
        s_sc[...] = jnp.zeros(s_sc.shape, F32)

    C = RET_CHUNK
    valid = jnp.clip(n_valid - c_idx * C, 0, C).astype(F32)
    valid_v = jnp.full((1, 1), valid, F32)
    n_row = lax.broadcasted_iota(jnp.int32, (C, 1), 0).astype(F32)
    n_col = lax.broadcasted_iota(jnp.int32, (1, C), 1).astype(F32)
    diff = n_row - n_col
    for h in range(RET_HEADS):
        lg = _ret_log_decay(h)
        q = q_ref[:, h * RET_DK:(h + 1) * RET_DK]
        kt = kt_ref[h * RET_DK:(h + 1) * RET_DK, :]
        v = v_ref[:, h * RET_DV:(h + 1) * RET_DV]
        dmask = jnp.where(diff >= 0, jnp.exp(lg * jnp.maximum(diff, 0.0)), 0.0)
        s = jnp.dot(q, kt, preferred_element_type=F32) * dmask
        inner = jnp.dot(s.astype(BF16), v, preferred_element_type=F32)
        state = s_sc[h]
        q_dec = (q.astype(F32) * jnp.exp(lg * (n_row + 1.0))).astype(BF16)
        cross = jnp.dot(q_dec, state.astype(BF16), preferred_element_type=F32)
        k_decay = jnp.where(n_col < valid, jnp.exp(lg * (valid - 1.0 - n_col)), 0.0)
        k_dec = (kt.astype(F32) * k_decay).astype(BF16)
        s_sc[h] = jnp.exp(lg * valid_v) * state + jnp.dot(k_dec, v, preferred_element_type=F32)
        sl = slice(h * RET_DV, (h + 1) * RET_DV)
        og_ref[:, sl] = _ret_gate_out(inner + cross, sg_ref[:, sl])

    @pl.when(c_idx == pl.num_programs(1) - 1)
    def _():
        st_ref[...] = s_sc[...]


def _ret_prompt(q, kt, v, sg, n_valid):
    B, F, _ = q.shape
    C = RET_CHUNK
    tok = lambda b, c: (b, c, 0)
    return pl.pallas_call(
        functools.partial(_ret_prompt_kernel, n_valid=n_valid),
        grid=(B, F // C),
        in_specs=[
            pl.BlockSpec((None, C, RET_QK), tok),
            pl.BlockSpec((None, RET_QK, C), lambda b, c: (b, 0, c)),
            pl.BlockSpec((None, C, RET_VD), tok),
            pl.BlockSpec((None, C, RET_VD), tok),
        ],
        out_specs=[
            pl.BlockSpec((None, C, RET_VD), tok),
            pl.BlockSpec((None, RET_HEADS, RET_DK, RET_DV), lambda b, c: (b, 0, 0, 0)),
        ],
        out_shape=[
            jax.ShapeDtypeStruct((B, F, RET_VD), BF16),
            jax.ShapeDtypeStruct((B, RET_HEADS, RET_DK, RET_DV), F32),
        ],
        scratch_shapes=[pltpu.VMEM((RET_HEADS, RET_DK, RET_DV), F32)],
        compiler_params=_params("parallel", "arbitrary"),
        name="ret_prompt",
    )(q, kt, v, sg)


def _ret_sample_kernel(q_ref, k_ref, v_ref, sg_ref, st_in_ref, og_ref, st_out_ref, *, n_new):
    T = n_new
    n_row = lax.broadcasted_iota(jnp.int32, (T, 1), 0).astype(F32)
    n_col = lax.broadcasted_iota(jnp.int32, (1, T), 1).astype(F32)
    diff = n_row - n_col
    for h in range(RET_HEADS):
        lg = _ret_log_decay(h)
        q = q_ref[:, h * RET_DK:(h + 1) * RET_DK].astype(F32)
        k = k_ref[:, h * RET_DK:(h + 1) * RET_DK].astype(F32)
        v = v_ref[:, h * RET_DV:(h + 1) * RET_DV].astype(F32)
        dmask = jnp.where(diff >= 0, jnp.exp(lg * jnp.maximum(diff, 0.0)), 0.0)
        s = lax.dot_general(q, k, _NT, preferred_element_type=F32) * dmask
        inner = jnp.dot(s, v, preferred_element_type=F32)
        state = st_in_ref[h]
        q_dec = (q * jnp.exp(lg * (n_row + 1.0))).astype(BF16)
        cross = jnp.dot(q_dec, state.astype(BF16), preferred_element_type=F32)
        k_dec = k * jnp.exp(lg * (T - 1.0 - n_row))
        st_out_ref[h] = math.exp(lg * T) * state + lax.dot_general(k_dec, v, _TN, preferred_element_type=F32)
        sl = slice(h * RET_DV, (h + 1) * RET_DV)
        og_ref[:, sl] = _ret_gate_out(inner + cross, sg_ref[:, sl])


def _ret_sample(q, k, v, sg, state_all, layer, prev):
    DB, T, _ = q.shape
    tok = lambda b: (b, 0, 0)
    st = pl.BlockSpec((None, None, RET_HEADS, RET_DK, RET_DV), lambda b: (layer, b, 0, 0, 0))
    n_in = 5
    in_specs = [
        pl.BlockSpec((None, T, RET_QK), tok),
        pl.BlockSpec((None, T, RET_QK), tok),
        pl.BlockSpec((None, T, RET_VD), tok),
        pl.BlockSpec((None, T, RET_VD), tok),
        st,
    ]
    body = functools.partial(_ret_sample_kernel, n_new=T)
    aliases, extra = {}, ()
    if prev is not None:
        body = _drop_refs(body, n_in, 1)
        in_specs.append(pl.BlockSpec(memory_space=pl.ANY))
        aliases = {n_in: 1}
        extra = (prev,)
    return pl.pallas_call(
        body,
        grid=(DB,),
        in_specs=in_specs,
        out_specs=[pl.BlockSpec((None, T, RET_VD), tok), st],
        out_shape=[
            jax.ShapeDtypeStruct((DB, T, RET_VD), BF16),
            jax.ShapeDtypeStruct(state_all.shape, F32),
        ],
        input_output_aliases=aliases,
        compiler_params=_params("parallel"),
        name="ret_sample",
    )(q, k, v, sg, state_all, *extra)


def kernel(x_prompt, x_sample, cache_k, cache_v, state_ret, page_table, meta_tokens, norm_mix, norm_ffn, norm_out, da_w_in, da_w_out, da_lambda_q1, da_lambda_k1, da_lambda_q2, da_lambda_k2, da_subln, ret_w_in, ret_w_out, ffn_w_in, ffn_w_out):
    B, seq, _ = x_prompt.shape
    DB, T, _ = x_sample.shape
    depth = norm_mix.shape[0]
    L = N_META + seq
    NS = DB * T
    page = cache_k.shape[2]
    past = page_table.shape[1] * page

    tp = _tiles(L)
    ts = {k: min(t, NS) for k, t in _tiles(NS).items()}
    F = _round_up(L, tp["frame"])

    meta = jnp.broadcast_to(meta_tokens.astype(F32)[None], (B, N_META, D_MODEL))
    hp = jnp.concatenate([meta, x_prompt, jnp.zeros((B, F - L, D_MODEL), F32)], axis=1)
    hs = x_sample.reshape(1, NS, D_MODEL)

    pos_p = jnp.arange(F)
    pos_s = past + jnp.arange(NS) % T
    da_tabs_p, da_tabs_s = _da_rope_tables(pos_p), _da_rope_tables(pos_s)
    ret_tabs_p, ret_tabs_s = _ret_rot_tables(pos_p), _ret_rot_tables(pos_s)

    kc = jnp.transpose(cache_k, (0, 1, 3, 4, 5, 2)).reshape(cache_k.shape[0], cache_k.shape[1], DA_HEADS, 2 * DA_DK, page)
    vc = cache_v.reshape(cache_v.shape[0], cache_v.shape[1], page * DA_HEADS, DA_DV)

    row = lambda a: a.astype(F32).reshape(1, -1)
    st_s_all = None
    kp_rows, vp_rows, ks_rows, vs_rows, st_p_list = [], [], [], [], []
    for i in range(depth):
        j = i // N_MIXERS
        g_mix = row(norm_mix[i])
        if i % N_MIXERS == 0:
            lam_init = 0.8 - 0.6 * math.exp(-0.3 * i)
            w = da_w_in[j]
            wq = w[:, :D_MODEL].astype(BF16)
            wkt = w[:, D_MODEL:2 * D_MODEL].T.astype(BF16)
            wv = w[:, 2 * D_MODEL:].astype(BF16)
            lams = (row(da_lambda_q1[j]), row(da_lambda_k1[j]), row(da_lambda_q2[j]), row(da_lambda_k2[j]))
            g_sub = row(da_subln[j])
            qd, ktf, ktb, vf, vb = _da_inproj(hp, g_mix, wq, wkt, wv, da_tabs_p, L, tp["tm_proj"])
            qd_s, ktf_s, _, vf_s, _ = _da_inproj(hs, g_mix, wq, wkt, wv, da_tabs_s, NS, ts["tm_proj"])
            op = _da_prompt(qd, ktb, vb, lams, g_sub, lam_init, tp["tq"], tp["tk"], tp["td"])
            q2 = qd_s.reshape(2, DB, T, D_MODEL).transpose(1, 0, 2, 3).reshape(DB, 2 * T, D_MODEL)
            kn = ktf_s[0].T.reshape(DB, T, D_MODEL)
            vn = vf_s.reshape(DB, T, DA_HEADS, DA_DV)
            osm = _da_sample(page_table, j, q2, kc, vc, kn, vn, lams, g_sub, lam_init)
            o_p, o_s = op.reshape(B * F, D_MODEL), osm.reshape(NS, D_MODEL)
            wo = da_w_out[j].astype(BF16)
            kp_rows.append(ktf.reshape(B, DA_HEADS, 2, DA_DK, L).transpose(0, 4, 1, 2, 3))
            vp_rows.append(vf)
            ks_rows.append(ktf_s.reshape(DA_HEADS, 2, DA_DK, DB, T).transpose(3, 4, 0, 1, 2))
            vs_rows.append(vn)
        else:
            w = ret_w_in[j]
            wq = w[:, :RET_QK].astype(BF16)
            wkt = w[:, RET_QK:2 * RET_QK].T.astype(BF16)
            wv = w[:, 2 * RET_QK:2 * RET_QK + RET_VD].astype(BF16)
            wg = w[:, 2 * RET_QK + RET_VD:].astype(BF16)
            q, kt, v, sg = _ret_inproj(hp, g_mix, wq, wkt, wv, wg, ret_tabs_p, tp["tm_proj"])
            q_s, kt_s, v_s, sg_s = _ret_inproj(hs, g_mix, wq, wkt, wv, wg, ret_tabs_s, ts["tm_proj"])
            og, st_p = _ret_prompt(q, kt, v, sg, L)
            per_seq = lambda a: a.reshape(DB, T, a.shape[-1])
            og_s, st_s_all = _ret_sample(per_seq(q_s[0]), per_seq(kt_s[0].T), per_seq(v_s[0]), per_seq(sg_s[0]),
                                         state_ret, j, st_s_all)
            o_p, o_s = og.reshape(B * F, RET_VD), og_s.reshape(NS, RET_VD)
            wo = ret_w_out[j].astype(BF16)
            st_p_list.append(st_p)
        last = i == depth - 1
        ffn = (wo, row(norm_ffn[i]), ffn_w_in[i].astype(BF16), ffn_w_out[i].astype(BF16), row(norm_out))
        hp = _outproj_ffn(hp.reshape(B * F, D_MODEL), o_p, *ffn, tp["tm_ffn"], last).reshape(B, F, D_MODEL)
        hs = _outproj_ffn(hs.reshape(NS, D_MODEL), o_s, *ffn, ts["tm_ffn"], last).reshape(1, NS, D_MODEL)
    y_prompt = hp[:, N_META:L]
    y_sample = hs.reshape(DB, T, D_MODEL)
    return (y_prompt, y_sample, jnp.stack(kp_rows), jnp.stack(vp_rows), jnp.stack(ks_rows),
            jnp.stack(vs_rows), jnp.stack(st_p_list), st_s_all)
```

```python
import functools
import math

import jax
import jax.numpy as jnp
from jax import lax
from jax.experimental import pallas as pl
from jax.experimental.pallas import tpu as pltpu

F32, BF16 = jnp.float32, jnp.bfloat16

D_MODEL = 1024
N_META = 16
N_MIXERS = 2
DA_HEADS = 8
DA_DK = 64
DA_DV = 128
ROPE_THETA = 500000.0
ROPE_DIMS = 16
RET_HEADS = 4
RET_DK = 256
RET_DV = 512
RET_QK = RET_HEADS * RET_DK
RET_VD = RET_HEADS * RET_DV
RET_CHUNK = 256
RET_THETA = 10000.0
FFN_HIDDEN = 2816
FFN_CHUNK = 256
RMS_EPS = 1e-6
MASK_VALUE = -1e30
LANES = 128
VMEM_LIMIT = 56 * 1024 * 1024

_NT = (((1,), (1,)), ((), ()))
_TN = (((0,), (0,)), ((), ()))


def _tiles(n_rows):
    if n_rows >= 4096:
        return dict(frame=768, tm_proj=384, tm_ffn=768, tq=768, tk=768, td=256)
    if n_rows >= 512:
        return dict(frame=512, tm_proj=512, tm_ffn=512, tq=512, tk=256, td=256)
    return dict(frame=256, tm_proj=128, tm_ffn=128, tq=256, tk=128, td=128)


def _round_up(x, m):
    return -(-x // m) * m


def _rms(x):
    return x * lax.rsqrt(jnp.mean(x * x, axis=-1, keepdims=True) + RMS_EPS)


def _resident(shape):
    return pl.BlockSpec(shape, lambda *_: (0,) * len(shape), pipeline_mode=pl.Buffered(1))


def _params(*sem):
    return pltpu.CompilerParams(dimension_semantics=sem, vmem_limit_bytes=VMEM_LIMIT)


def _da_inproj_kernel(x_ref, g_ref, wq_ref, wkt_ref, wv_ref, c_ref, s1_ref, s2_ref, ct_ref, st_ref, *rest,
                      n_prev):
    if n_prev:
        ktf_prev_ref, vf_prev_ref, qd_ref, ktf_all_ref, ktb_ref, vf_all_ref, vb_ref = rest
        ktf_all_ref[:n_prev] = ktf_prev_ref[...]
        vf_all_ref[:n_prev] = vf_prev_ref[...]
    else:
        qd_ref, ktf_all_ref, ktb_ref, vf_all_ref, vb_ref = rest
    ktf_ref, vf_ref = ktf_all_ref.at[n_prev], vf_all_ref.at[n_prev]
    hn = (_rms(x_ref[...]) * g_ref[...]).astype(BF16)
    lane = lax.broadcasted_iota(jnp.int32, (1, LANES), 1)
    comp0 = (lane < DA_DK).astype(F32)
    comp1 = 1.0 - comp0
    q = jnp.dot(hn, wq_ref[...], preferred_element_type=F32)
    c, s1, s2 = c_ref[...], s1_ref[...], s2_ref[...]
    for h in range(DA_HEADS):
        sl = slice(h * LANES, (h + 1) * LANES)
        qh = q[:, sl]
        qr = qh * c + pltpu.roll(qh, ROPE_DIMS // 2, 1) * s1 + pltpu.roll(qh, LANES - ROPE_DIMS // 2, 1) * s2
        qd_ref[0, :, sl] = (qr * comp0).astype(BF16)
        qd_ref[1, :, sl] = (qr * comp1).astype(BF16)
    kt = lax.dot_general(wkt_ref[...], hn, _NT, preferred_element_type=F32)
    ct, st = ct_ref[...], st_ref[...]
    half = ROPE_DIMS // 2
    for grp in range(2 * DA_HEADS):
        b0 = grp * DA_DK
        x1 = kt[b0:b0 + half]
        x2 = kt[b0 + half:b0 + 2 * half]
        tail = kt[b0 + 2 * half:b0 + DA_DK]
        o1 = x1 * ct - x2 * st
        o2 = x2 * ct + x1 * st
        for lo, val in ((b0, o1), (b0 + half, o2), (b0 + 2 * half, tail)):
            ktf_ref[lo:lo + val.shape[0], :] = val
            ktb_ref[lo:lo + val.shape[0], :] = val.astype(BF16)
    v = jnp.dot(hn, wv_ref[...], preferred_element_type=F32)
    vb_ref[...] = v.astype(BF16)
    for h in range(DA_HEADS):
        vf_ref[:, h, :] = v[:, h * DA_DV:(h + 1) * DA_DV]


def _drop_refs(body, start, count):
    def wrapped(*refs):
        return body(*refs[:start], *refs[start + count:])
    return wrapped


def _da_inproj(x, g, wq, wkt, wv, tabs, n_valid, tm, prev):
    B, F, _ = x.shape
    c, s1, s2, ct, st = tabs
    n_prev = 0 if prev is None else prev[0].shape[0]
    tok = lambda b, i: (b, i, 0)
    kt_spec = lambda n: pl.BlockSpec((n, None, D_MODEL, tm), lambda b, i: (0, b, 0, i))
    v_spec = lambda n: pl.BlockSpec((n, None, tm, DA_HEADS, DA_DV), lambda b, i: (0, b, i, 0, 0))
    in_specs = [
        pl.BlockSpec((None, tm, D_MODEL), tok),
        _resident((1, D_MODEL)),
        _resident((D_MODEL, D_MODEL)),
        _resident((D_MODEL, D_MODEL)),
        _resident((D_MODEL, D_MODEL)),
        pl.BlockSpec((tm, LANES), lambda b, i: (i, 0)),
        pl.BlockSpec((tm, LANES), lambda b, i: (i, 0)),
        pl.BlockSpec((tm, LANES), lambda b, i: (i, 0)),
        pl.BlockSpec((ROPE_DIMS // 2, tm), lambda b, i: (0, i)),
        pl.BlockSpec((ROPE_DIMS // 2, tm), lambda b, i: (0, i)),
    ]
    if n_prev:
        in_specs += [kt_spec(n_prev), v_spec(n_prev)]
    qd, ktf, ktb, vf, vb = pl.pallas_call(
        functools.partial(_da_inproj_kernel, n_prev=n_prev),
        grid=(B, F // tm),
        in_specs=in_specs,
        out_specs=[
            pl.BlockSpec((2, None, tm, D_MODEL), lambda b, i: (0, b, i, 0)),
            kt_spec(n_prev + 1),
            pl.BlockSpec((None, D_MODEL, tm), lambda b, i: (b, 0, i)),
            v_spec(n_prev + 1),
            pl.BlockSpec((None, tm, D_MODEL), tok),
        ],
        out_shape=[
            jax.ShapeDtypeStruct((2, B, F, D_MODEL), BF16),
            jax.ShapeDtypeStruct((n_prev + 1, B, D_MODEL, n_valid), F32),
            jax.ShapeDtypeStruct((B, D_MODEL, F), BF16),
            jax.ShapeDtypeStruct((n_prev + 1, B, n_valid, DA_HEADS, DA_DV), F32),
            jax.ShapeDtypeStruct((B, F, D_MODEL), BF16),
        ],
        compiler_params=_params("parallel", "parallel"),
        name="da_inproj",
    )(x, g, wq, wkt, wv, c, s1, s2, ct, st, *(prev or ()))
    return qd, ktb, vb, (ktf, vf)


def _da_rope_tables(pos):
    half = ROPE_DIMS // 2
    inv = ROPE_THETA ** (-jnp.arange(0, ROPE_DIMS, 2, dtype=F32) / ROPE_DIMS)
    ang = pos.astype(F32)[:, None] * inv[None, :]
    cos, sin = jnp.cos(ang), jnp.sin(ang)
    n = pos.shape[0]
    zeros = lambda w: jnp.zeros((n, w), F32)
    c64 = jnp.concatenate([cos, cos, jnp.ones((n, DA_DK - ROPE_DIMS), F32)], axis=1)
    s1_64 = jnp.concatenate([zeros(half), sin, zeros(DA_DK - ROPE_DIMS)], axis=1)
    s2_64 = jnp.concatenate([-sin, zeros(DA_DK - half)], axis=1)
    scale = DA_DK ** -0.5 * math.log2(math.e)
    tile2 = lambda a: jnp.concatenate([a, a], axis=1) * scale
    return tile2(c64), tile2(s1_64), tile2(s2_64), cos.T, sin.T


def _da_lambda(lq1_ref, lk1_ref, lq2_ref, lk2_ref, lam_init):
    a = jnp.sum(lq1_ref[...] * lk1_ref[...], axis=-1, keepdims=True)
    b = jnp.sum(lq2_ref[...] * lk2_ref[...], axis=-1, keepdims=True)
    return jnp.exp(a) - jnp.exp(b) + lam_init


def _da_finish(acc0, l0, acc1, l1, lam, g, lam_init):
    o = acc0 / l0 - lam * (acc1 / l1)
    return _rms(o) * g * (1.0 - lam_init)


def _da_prompt_kernel(q_ref, kt_ref, v_ref, lq1_ref, lk1_ref, lq2_ref, lk2_ref, g_ref, o_ref,
                      m_sc, l_sc, acc_sc, *, tq, tk, td, rows_last, lam_init):
    i = pl.program_id(2)
    m_sc[...] = jnp.full(m_sc.shape, MASK_VALUE, F32)
    l_sc[...] = jnp.zeros(l_sc.shape, F32)
    acc_sc[...] = jnp.zeros(acc_sc.shape, F32)
    n_full = i * (tq // tk)

    def run(rows):
        def step(off, r0, width, masked):
            kt = kt_ref[:, pl.ds(off, width)]
            v = v_ref[pl.ds(off, width), :]
            for c in range(2):
                s = jnp.dot(q_ref[c, r0:rows, :], kt, preferred_element_type=F32)
                if masked:
                    qpos = i * tq + r0 + lax.broadcasted_iota(jnp.int32, s.shape, 0)
                    kpos = off + lax.broadcasted_iota(jnp.int32, s.shape, 1)
                    s = jnp.where(kpos <= qpos, s, MASK_VALUE)
                tiles = [s[:, t * LANES:(t + 1) * LANES] for t in range(width // LANES)]
                m_old = m_sc[c, r0:rows, :]
                blk_max = jnp.max(functools.reduce(jnp.maximum, tiles), axis=-1, keepdims=True)
                m_new = jnp.maximum(m_old, blk_max)
                alpha = jnp.exp2(m_old - m_new)
                ps = [jnp.exp2(t - m_new) for t in tiles]
                l_sc[c, r0:rows, :] = alpha * l_sc[c, r0:rows, :] + functools.reduce(jnp.add, ps)
                p = jnp.concatenate(ps, axis=1).astype(BF16)
                acc_sc[c, r0:rows, :] = alpha * acc_sc[c, r0:rows, :] + jnp.dot(p, v, preferred_element_type=F32)
                m_sc[c, r0:rows, :] = m_new

        def pair_body(j, carry):
            step(pl.multiple_of(2 * j * tk, tk), 0, 2 * tk, False)
            return carry

        lax.fori_loop(0, n_full // 2, pair_body, 0)

        @pl.when(n_full % 2 == 1)
        def _():
            step(pl.multiple_of((n_full - 1) * tk, tk), 0, tk, False)

        for d in range(-(-rows // td)):
            step(pl.multiple_of(i * tq + d * td, td), d * td, td, True)

        lam = _da_lambda(lq1_ref, lk1_ref, lq2_ref, lk2_ref, lam_init)
        l0 = jnp.sum(l_sc[0, :rows, :], axis=-1, keepdims=True)
        l1 = jnp.sum(l_sc[1, :rows, :], axis=-1, keepdims=True)
        o = _da_finish(acc_sc[0, :rows, :], l0, acc_sc[1, :rows, :], l1, lam, g_ref[...], lam_init)
        o_ref[:rows, :] = o.astype(o_ref.dtype)
        if rows < tq:
            o_ref[rows:, :] = jnp.zeros((tq - rows, LANES), o_ref.dtype)

    if rows_last == tq:
        run(tq)
    else:
        last = pl.num_programs(2) - 1
        pl.when(i < last)(lambda: run(tq))
        pl.when(i == last)(lambda: run(rows_last))


def _da_prompt(qd, ktb, vb, lams, g, lam_init, n_valid, tq, tk, td):
    _, B, F, _ = qd.shape
    rows_last = _round_up(n_valid - (F // tq - 1) * tq, 16)
    lam_spec = pl.BlockSpec((1, DA_DK), lambda b, h, i: (0, 0))
    return pl.pallas_call(
        functools.partial(_da_prompt_kernel, tq=tq, tk=tk, td=td, rows_last=rows_last, lam_init=lam_init),
        grid=(B, DA_HEADS, F // tq),
        in_specs=[
            pl.BlockSpec((2, None, tq, LANES), lambda b, h, i: (0, b, i, h)),
            pl.BlockSpec((None, LANES, F), lambda b, h, i: (b, h, 0)),
            pl.BlockSpec((None, F, LANES), lambda b, h, i: (b, 0, h)),
            lam_spec, lam_spec, lam_spec, lam_spec,
            pl.BlockSpec((1, DA_DV), lambda b, h, i: (0, 0)),
        ],
        out_specs=pl.BlockSpec((None, tq, LANES), lambda b, h, i: (b, i, h)),
        out_shape=jax.ShapeDtypeStruct((B, F, D_MODEL), BF16),
        scratch_shapes=[
            pltpu.VMEM((2, tq, LANES), F32),
            pltpu.VMEM((2, tq, LANES), F32),
            pltpu.VMEM((2, tq, LANES), F32),
        ],
        compiler_params=_params("parallel", "parallel", "parallel"),
        name="da_prompt_attn",
    )(qd, ktb, vb, *lams, g)


def _da_sample_kernel(pt_ref, q_ref, kn_ref, vn_ref, lq1_ref, lk1_ref, lq2_ref, lk2_ref, g_ref, *rest,
                      n_pages, n_new, lam_init):
    del pt_ref
    kc_refs, vc_refs, o_ref = rest[:n_pages], rest[n_pages:2 * n_pages], rest[2 * n_pages]
    page = vc_refs[0].shape[0] // DA_HEADS
    lam = _da_lambda(lq1_ref, lk1_ref, lq2_ref, lk2_ref, lam_init)
    q = q_ref[...]
    row = lax.broadcasted_iota(jnp.int32, (2 * n_new, 1), 0)
    step = jnp.where(row >= n_new, row - n_new, row)
    for h in range(DA_HEADS):
        sl = slice(h * LANES, (h + 1) * LANES)
        qh = q[:, sl]
        kt = jnp.concatenate([r[h].astype(BF16) for r in kc_refs], axis=1)
        s = jnp.dot(qh, kt, preferred_element_type=F32)
        tiles = [s[:, t * LANES:(t + 1) * LANES] for t in range(s.shape[1] // LANES)]
        qf = qh.astype(F32)
        kn = kn_ref[:, sl]
        s_new = [jnp.where(step >= j, jnp.sum(qf * kn[j:j + 1, :], axis=-1, keepdims=True), MASK_VALUE)
                 for j in range(n_new)]
        m = jnp.max(functools.reduce(jnp.maximum, tiles), axis=-1, keepdims=True)
        m = functools.reduce(jnp.maximum, s_new, m)
        ps = [jnp.exp2(t - m) for t in tiles]
        l = jnp.sum(functools.reduce(jnp.add, ps), axis=-1, keepdims=True)
        v = jnp.concatenate([r[pl.ds(h, page, stride=DA_HEADS), :].astype(BF16) for r in vc_refs], axis=0)
        acc = jnp.dot(jnp.concatenate(ps, axis=1).astype(BF16), v, preferred_element_type=F32)
        for j in range(n_new):
            pj = jnp.exp2(s_new[j] - m)
            l = l + pj
            acc = acc + pj * vn_ref[j:j + 1, h, :]
        o_ref[:, sl] = _da_finish(acc[:n_new], l[:n_new], acc[n_new:], l[n_new:], lam, g_ref[...], lam_init)


def _da_sample(page_table, layer, q2, kc, vc, kn, vn, lams, g, lam_init):
    DB, n_pages = page_table.shape
    T = kn.shape[1]
    page = kc.shape[-1]
    lam_spec = pl.BlockSpec((1, DA_DK), lambda b, pt: (0, 0))
    k_spec = lambda r: pl.BlockSpec((None, None, DA_HEADS, LANES, page), lambda b, pt: (layer, pt[b, r], 0, 0, 0))
    v_spec = lambda r: pl.BlockSpec((None, None, page * DA_HEADS, DA_DV), lambda b, pt: (layer, pt[b, r], 0, 0))
    grid_spec = pltpu.PrefetchScalarGridSpec(
        num_scalar_prefetch=1,
        grid=(DB,),
        in_specs=[
            pl.BlockSpec((None, 2 * T, D_MODEL), lambda b, pt: (b, 0, 0)),
            pl.BlockSpec((None, T, D_MODEL), lambda b, pt: (b, 0, 0)),
            pl.BlockSpec((None, T, DA_HEADS, DA_DV), lambda b, pt: (b, 0, 0, 0)),
            lam_spec, lam_spec, lam_spec, lam_spec,
            pl.BlockSpec((1, DA_DV), lambda b, pt: (0, 0)),
        ] + [k_spec(r) for r in range(n_pages)] + [v_spec(r) for r in range(n_pages)],
        out_specs=pl.BlockSpec((None, T, D_MODEL), lambda b, pt: (b, 0, 0)),
    )
    return pl.pallas_call(
        functools.partial(_da_sample_kernel, n_pages=n_pages, n_new=T, lam_init=lam_init),
        grid_spec=grid_spec,
        out_shape=jax.ShapeDtypeStruct((DB, T, D_MODEL), F32),
        compiler_params=_params("parallel"),
        name="da_sample_attn",
    )(page_table, q2, kn, vn, *lams, g, *([kc] * n_pages), *([vc] * n_pages))


def _ffn_kernel(x_ref, o_ref, wo_ref, g_ref, win_ref, wout_ref, gout_ref, y_ref, acc_sc, *, final_norm):
    h = x_ref[...] + jnp.dot(o_ref[...].astype(BF16), wo_ref[...], preferred_element_type=F32)
    hn = (_rms(h) * g_ref[...]).astype(BF16)
    acc_sc[...] = h
    for j in range(FFN_HIDDEN // FFN_CHUNK):
        lo = j * FFN_CHUNK
        gate = jnp.dot(hn, win_ref[:, lo:lo + FFN_CHUNK], preferred_element_type=F32)
        up = jnp.dot(hn, win_ref[:, FFN_HIDDEN + lo:FFN_HIDDEN + lo + FFN_CHUNK], preferred_element_type=F32)
        act = (gate * jax.nn.sigmoid(gate) * up).astype(BF16)
        acc_sc[...] += jnp.dot(act, wout_ref[lo:lo + FFN_CHUNK, :], preferred_element_type=F32)
    y = acc_sc[...]
    if final_norm:
        y = _rms(y) * gout_ref[...]
    y_ref[...] = y


def _outproj_ffn(x, o, wo, g, win, wout, gout, tm, final_norm):
    N = x.shape[0]
    ko = o.shape[1]
    return pl.pallas_call(
        functools.partial(_ffn_kernel, final_norm=final_norm),
        grid=(N // tm,),
        in_specs=[
            pl.BlockSpec((tm, D_MODEL), lambda i: (i, 0)),
            pl.BlockSpec((tm, ko), lambda i: (i, 0)),
            _resident((ko, D_MODEL)),
            _resident((1, D_MODEL)),
            _resident((D_MODEL, 2 * FFN_HIDDEN)),
            _resident((FFN_HIDDEN, D_MODEL)),
            _resident((1, D_MODEL)),
        ],
        out_specs=pl.BlockSpec((tm, D_MODEL), lambda i: (i, 0)),
        out_shape=jax.ShapeDtypeStruct((N, D_MODEL), F32),
        scratch_shapes=[pltpu.VMEM((tm, D_MODEL), F32)],
        compiler_params=_params("parallel"),
        name="outproj_ffn",
    )(x, o, wo, g, win, wout, gout)


def _ret_inproj_kernel(x_ref, g_ref, wq_ref, wkt_ref, wv_ref, wg_ref, c_ref, a_ref, b_ref,
                       ct_ref, at_ref, bt_ref, q_ref, kt_ref, v_ref, sg_ref):
    hn = (_rms(x_ref[...]) * g_ref[...]).astype(BF16)
    q = jnp.dot(hn, wq_ref[...], preferred_element_type=F32)
    c, a, b = c_ref[...], a_ref[...], b_ref[...]
    for h in range(RET_HEADS):
        sl = slice(h * RET_DK, (h + 1) * RET_DK)
        qh = q[:, sl]
        qr = qh * c + pltpu.roll(qh, RET_DK - 1, 1) * a + pltpu.roll(qh, 1, 1) * b
        q_ref[:, sl] = qr.astype(BF16)
    kt = lax.dot_general(wkt_ref[...], hn, _NT, preferred_element_type=F32)
    ct, at, bt = ct_ref[...], at_ref[...], bt_ref[...]
    for h in range(RET_HEADS):
        sl = slice(h * RET_DK, (h + 1) * RET_DK)
        kh = kt[sl]
        kr = kh * ct + pltpu.roll(kh, RET_DK - 1, 0) * at + pltpu.roll(kh, 1, 0) * bt
        kt_ref[sl, :] = kr.astype(BF16)
    v_ref[...] = jnp.dot(hn, wv_ref[...], preferred_element_type=F32).astype(BF16)
    gate = jnp.dot(hn, wg_ref[...], preferred_element_type=F32)
    sg_ref[...] = (gate * jax.nn.sigmoid(gate)).astype(BF16)


def _ret_inproj(x, g, wq, wkt, wv, wg, tabs, tm):
    B, F, _ = x.shape
    c, a, b, ct, at, bt = tabs
    tok = lambda bb, i: (bb, i, 0)
    tab = pl.BlockSpec((tm, RET_DK), lambda bb, i: (i, 0))
    tab_t = pl.BlockSpec((RET_DK, tm), lambda bb, i: (0, i))
    return pl.pallas_call(
        _ret_inproj_kernel,
        grid=(B, F // tm),
        in_specs=[
            pl.BlockSpec((None, tm, D_MODEL), tok),
            _resident((1, D_MODEL)),
            _resident((D_MODEL, RET_QK)),
            _resident((RET_QK, D_MODEL)),
            _resident((D_MODEL, RET_VD)),
            _resident((D_MODEL, RET_VD)),
            tab, tab, tab, tab_t, tab_t, tab_t,
        ],
        out_specs=[
            pl.BlockSpec((None, tm, RET_QK), tok),
            pl.BlockSpec((None, RET_QK, tm), lambda bb, i: (bb, 0, i)),
            pl.BlockSpec((None, tm, RET_VD), tok),
            pl.BlockSpec((None, tm, RET_VD), tok),
        ],
        out_shape=[
            jax.ShapeDtypeStruct((B, F, RET_QK), BF16),
            jax.ShapeDtypeStruct((B, RET_QK, F), BF16),
            jax.ShapeDtypeStruct((B, F, RET_VD), BF16),
            jax.ShapeDtypeStruct((B, F, RET_VD), BF16),
        ],
        compiler_params=_params("parallel", "parallel"),
        name="ret_inproj",
    )(x, g, wq, wkt, wv, wg, c, a, b, ct, at, bt)


def _ret_rot_tables(pos):
    angle = 1.0 / (RET_THETA ** jnp.linspace(0.0, 1.0, RET_DK // 2, dtype=F32))
    angle = jnp.repeat(angle, 2)
    ang = pos.astype(F32)[:, None] * angle[None, :]
    cos, sin = jnp.cos(ang), jnp.sin(ang)
    even = (jnp.arange(RET_DK) % 2 == 0)[None, :]
    a = jnp.where(even, -sin, 0.0)
    b = jnp.where(even, 0.0, sin)
    ks = RET_DK ** -0.5
    return cos, a, b, (cos * ks).T, (a * ks).T, (b * ks).T


def _ret_log_decay(h):
    return math.log(1.0 - 2.0 ** (-5.0 - h))


def _ret_gate_out(o, sg):
    return (sg.astype(F32) * _rms(o)).astype(BF16)


def _ret_prompt_kernel(q_ref, kt_ref, v_ref, sg_ref, og_ref, st_ref, s_sc, *, n_valid):
    c_idx = pl.program_id(1)

    @pl.when(c_idx == 0)
    def _():
        s_sc[...] = jnp.zeros(s_sc.shape, F32)

    C = RET_CHUNK
    valid = jnp.clip(n_valid - c_idx * C, 0, C).astype(F32)
    valid_v = jnp.full((1, 1), valid, F32)
    n_row = lax.broadcasted_iota(jnp.int32, (C, 1), 0).astype(F32)
    n_col = lax.broadcasted_iota(jnp.int32, (1, C), 1).astype(F32)
    diff = n_row - n_col
    for h in range(RET_HEADS):
        lg = _ret_log_decay(h)
        q = q_ref[:, h * RET_DK:(h + 1) * RET_DK]
        kt = kt_ref[h * RET_DK:(h + 1) * RET_DK, :]
        v = v_ref[:, h * RET_DV:(h + 1) * RET_DV]
        dmask = jnp.where(diff >= 0, jnp.exp(lg * jnp.maximum(diff, 0.0)), 0.0)
        s = jnp.dot(q, kt, preferred_element_type=F32) * dmask
        inner = jnp.dot(s.astype(BF16), v, preferred_element_type=F32)
        state = s_sc[h]
        q_dec = (q.astype(F32) * jnp.exp(lg * (n_row + 1.0))).astype(BF16)
        cross = jnp.dot(q_dec, state.astype(BF16), preferred_element_type=F32)
        k_decay = jnp.where(n_col < valid, jnp.exp(lg * (valid - 1.0 - n_col)), 0.0)
        k_dec = (kt.astype(F32) * k_decay).astype(BF16)
        s_sc[h] = jnp.exp(lg * valid_v) * state + jnp.dot(k_dec, v, preferred_element_type=F32)
        sl = slice(h * RET_DV, (h + 1) * RET_DV)
        og_ref[:, sl] = _ret_gate_out(inner + cross, sg_ref[:, sl])

    @pl.when(c_idx == pl.num_programs(1) - 1)
    def _():
        st_ref[...] = s_sc[...]


def _ret_prompt(q, kt, v, sg, n_valid):
    B, F, _ = q.shape
    C = RET_CHUNK
    tok = lambda b, c: (b, c, 0)
    return pl.pallas_call(
        functools.partial(_ret_prompt_kernel, n_valid=n_valid),
        grid=(B, F // C),
        in_specs=[
            pl.BlockSpec((None, C, RET_QK), tok),
            pl.BlockSpec((None, RET_QK, C), lambda b, c: (b, 0, c)),
            pl.BlockSpec((None, C, RET_VD), tok),
            pl.BlockSpec((None, C, RET_VD), tok),
        ],
        out_specs=[
            pl.BlockSpec((None, C, RET_VD), tok),
            pl.BlockSpec((None, RET_HEADS, RET_DK, RET_DV), lambda b, c: (b, 0, 0, 0)),
        ],
        out_shape=[
            jax.ShapeDtypeStruct((B, F, RET_VD), BF16),
            jax.ShapeDtypeStruct((B, RET_HEADS, RET_DK, RET_DV), F32),
        ],
        scratch_shapes=[pltpu.VMEM((RET_HEADS, RET_DK, RET_DV), F32)],
        compiler_params=_params("parallel", "arbitrary"),
        name="ret_prompt",
    )(q, kt, v, sg)


def _ret_sample_kernel(q_ref, k_ref, v_ref, sg_ref, st_in_ref, og_ref, st_out_ref, *, n_new):
    T = n_new
    n_row = lax.broadcasted_iota(jnp.int32, (T, 1), 0).astype(F32)
    n_col = lax.broadcasted_iota(jnp.int32, (1, T), 1).astype(F32)
    diff = n_row - n_col
    for h in range(RET_HEADS):
        lg = _ret_log_decay(h)
        q = q_ref[:, h * RET_DK:(h + 1) * RET_DK].astype(F32)
        k = k_ref[:, h * RET_DK:(h + 1) * RET_DK].astype(F32)
        v = v_ref[:, h * RET_DV:(h + 1) * RET_DV].astype(F32)
        dmask = jnp.where(diff >= 0, jnp.exp(lg * jnp.maximum(diff, 0.0)), 0.0)
        s = lax.dot_general(q, k, _NT, preferred_element_type=F32) * dmask
        inner = jnp.dot(s, v, preferred_element_type=F32)
        state = st_in_ref[h]
        q_dec = (q * jnp.exp(lg * (n_row + 1.0))).astype(BF16)
        cross = jnp.dot(q_dec, state.astype(BF16), preferred_element_type=F32)
        k_dec = k * jnp.exp(lg * (T - 1.0 - n_row))
        st_out_ref[h] = math.exp(lg * T) * state + lax.dot_general(k_dec, v, _TN, preferred_element_type=F32)
        sl = slice(h * RET_DV, (h + 1) * RET_DV)
        og_ref[:, sl] = _ret_gate_out(inner + cross, sg_ref[:, sl])


def _ret_sample(q, k, v, sg, state_all, layer, prev):
    DB, T, _ = q.shape
    tok = lambda b: (b, 0, 0)
    st = pl.BlockSpec((None, None, RET_HEADS, RET_DK, RET_DV), lambda b: (layer, b, 0, 0, 0))
    n_in = 5
    in_specs = [
        pl.BlockSpec((None, T, RET_QK), tok),
        pl.BlockSpec((None, T, RET_QK), tok),
        pl.BlockSpec((None, T, RET_VD), tok),
        pl.BlockSpec((None, T, RET_VD), tok),
        st,
    ]
    body = functools.partial(_ret_sample_kernel, n_new=T)
    aliases, extra = {}, ()
    if prev is not None:
        body = _drop_refs(body, n_in, 1)
        in_specs.append(pl.BlockSpec(memory_space=pl.ANY))
        aliases = {n_in: 1}
        extra = (prev,)
    return pl.pallas_call(
        body,
        grid=(DB,),
        in_specs=in_specs,
        out_specs=[pl.BlockSpec((None, T, RET_VD), tok), st],
        out_shape=[
            jax.ShapeDtypeStruct((DB, T, RET_VD), BF16),
            jax.ShapeDtypeStruct(state_all.shape, F32),
        ],
        input_output_aliases=aliases,
        compiler_params=_params("parallel"),
        name="ret_sample",
    )(q, k, v, sg, state_all, *extra)


def kernel(x_prompt, x_sample, cache_k, cache_v, state_ret, page_table, meta_tokens, norm_mix, norm_ffn, norm_out, da_w_in, da_w_out, da_lambda_q1, da_lambda_k1, da_lambda_q2, da_lambda_k2, da_subln, ret_w_in, ret_w_out, ffn_w_in, ffn_w_out):
    B, seq, _ = x_prompt.shape
    DB, T, _ = x_sample.shape
    depth = norm_mix.shape[0]
    L = N_META + seq
    NS = DB * T
    page = cache_k.shape[2]
    past = page_table.shape[1] * page

    tp = _tiles(L)
    ts = {k: min(t, NS) for k, t in _tiles(NS).items()}
    F = _round_up(L, tp["frame"])

    meta = jnp.broadcast_to(meta_tokens.astype(F32)[None], (B, N_META, D_MODEL))
    hp = jnp.concatenate([meta, x_prompt, jnp.zeros((B, F - L, D_MODEL), F32)], axis=1)
    hs = x_sample.reshape(1, NS, D_MODEL)

    pos_p = jnp.arange(F)
    pos_s = past + jnp.arange(NS) % T
    da_tabs_p, da_tabs_s = _da_rope_tables(pos_p), _da_rope_tables(pos_s)
    ret_tabs_p, ret_tabs_s = _ret_rot_tables(pos_p), _ret_rot_tables(pos_s)

    kc = jnp.transpose(cache_k, (0, 1, 3, 4, 5, 2)).reshape(cache_k.shape[0], cache_k.shape[1], DA_HEADS, 2 * DA_DK, page)
    vc = cache_v.reshape(cache_v.shape[0], cache_v.shape[1], page * DA_HEADS, DA_DV)

    row = lambda a: a.astype(F32).reshape(1, -1)
    st_s_all = None
    kv_p = kv_s = None
    st_p_list = []
    for i in range(depth):
        j = i // N_MIXERS
        g_mix = row(norm_mix[i])
        if i % N_MIXERS == 0:
            lam_init = 0.8 - 0.6 * math.exp(-0.3 * i)
            w = da_w_in[j]
            wq = w[:, :D_MODEL].astype(BF16)
            wkt = w[:, D_MODEL:2 * D_MODEL].T.astype(BF16)
            wv = w[:, 2 * D_MODEL:].astype(BF16)
            lams = (row(da_lambda_q1[j]), row(da_lambda_k1[j]), row(da_lambda_q2[j]), row(da_lambda_k2[j]))
            g_sub = row(da_subln[j])
            qd, ktb, vb, kv_p = _da_inproj(hp, g_mix, wq, wkt, wv, da_tabs_p, L, tp["tm_proj"], kv_p)
            qd_s, _, _, kv_s = _da_inproj(hs, g_mix, wq, wkt, wv, da_tabs_s, NS, ts["tm_proj"], kv_s)
            op = _da_prompt(qd, ktb, vb, lams, g_sub, lam_init, L, tp["tq"], tp["tk"], tp["td"])
            q2 = qd_s.reshape(2, DB, T, D_MODEL).transpose(1, 0, 2, 3).reshape(DB, 2 * T, D_MODEL)
            kn = kv_s[0][-1, 0].T.reshape(DB, T, D_MODEL)
            vn = kv_s[1][-1].reshape(DB, T, DA_HEADS, DA_DV)
            osm = _da_sample(page_table, j, q2, kc, vc, kn, vn, lams, g_sub, lam_init)
            o_p, o_s = op.reshape(B * F, D_MODEL), osm.reshape(NS, D_MODEL)
            wo = da_w_out[j].astype(BF16)
        else:
            w = ret_w_in[j]
            wq = w[:, :RET_QK].astype(BF16)
            wkt = w[:, RET_QK:2 * RET_QK].T.astype(BF16)
            wv = w[:, 2 * RET_QK:2 * RET_QK + RET_VD].astype(BF16)
            wg = w[:, 2 * RET_QK + RET_VD:].astype(BF16)
            q, kt, v, sg = _ret_inproj(hp, g_mix, wq, wkt, wv, wg, ret_tabs_p, tp["tm_proj"])
            q_s, kt_s, v_s, sg_s = _ret_inproj(hs, g_mix, wq, wkt, wv, wg, ret_tabs_s, ts["tm_proj"])
            og, st_p = _ret_prompt(q, kt, v, sg, L)
            per_seq = lambda a: a.reshape(DB, T, a.shape[-1])
            og_s, st_s_all = _ret_sample(per_seq(q_s[0]), per_seq(kt_s[0].T), per_seq(v_s[0]), per_seq(sg_s[0]),
                                         state_ret, j, st_s_all)
            o_p, o_s = og.reshape(B * F, RET_VD), og_s.reshape(NS, RET_VD)
            wo = ret_w_out[j].astype(BF16)
            st_p_list.append(st_p)
        last = i == depth - 1
        ffn = (wo, row(norm_ffn[i]), ffn_w_in[i].astype(BF16), ffn_w_out[i].astype(BF16), row(norm_out))
        hp = _outproj_ffn(hp.reshape(B * F, D_MODEL), o_p, *ffn, tp["tm_ffn"], last).reshape(B, F, D_MODEL)
        hs = _outproj_ffn(hs.reshape(NS, D_MODEL), o_s, *ffn, ts["tm_ffn"], last).reshape(1, NS, D_MODEL)
    y_prompt = hp[:, N_META:L]
    y_sample = hs.reshape(DB, T, D_MODEL)
    n_att = kv_p[0].shape[0]
    new_k_prompt = kv_p[0].reshape(n_att, B, DA_HEADS, 2, DA_DK, L).transpose(0, 1, 5, 2, 3, 4)
    new_k_sample = kv_s[0].reshape(n_att, DA_HEADS, 2, DA_DK, DB, T).transpose(0, 4, 5, 1, 2, 3)
    new_v_sample = kv_s[1].reshape(n_att, DB, T, DA_HEADS, DA_DV)
    return (y_prompt, y_sample, new_k_prompt, kv_p[1], new_k_sample, new_v_sample,
            jnp.stack(st_p_list), st_s_all)
```

```python
import functools
import math

import jax
import jax.numpy as jnp
from jax import lax
from jax.experimental import pallas as pl
from jax.experimental.pallas import tpu as pltpu

F32, BF16 = jnp.float32, jnp.bfloat16

D_MODEL = 1024
N_META = 16
N_MIXERS = 2
DA_HEADS = 8
DA_DK = 64
DA_DV = 128
ROPE_THETA = 500000.0
ROPE_DIMS = 16
RET_HEADS = 4
RET_DK = 256
RET_DV = 512
RET_QK = RET_HEADS * RET_DK
RET_VD = RET_HEADS * RET_DV
RET_CHUNK = 256
RET_THETA = 10000.0
FFN_HIDDEN = 2816
FFN_CHUNK = 256
RMS_EPS = 1e-6
MASK_VALUE = -1e30
LANES = 128
VMEM_LIMIT = 60 * 1024 * 1024

_NT = (((1,), (1,)), ((), ()))
_TN = (((0,), (0,)), ((), ()))


def _tiles(n_rows):
    if n_rows >= 4096:
        return dict(frame=768, tm_proj=384, tm_ffn=768, tq=768, tk=768, td=256)
    if n_rows >= 512:
        return dict(frame=512, tm_proj=512, tm_ffn=512, tq=512, tk=256, td=256)
    return dict(frame=256, tm_proj=128, tm_ffn=128, tq=256, tk=128, td=128)


def _round_up(x, m):
    return -(-x // m) * m


def _rms(x):
    return x * lax.rsqrt(jnp.mean(x * x, axis=-1, keepdims=True) + RMS_EPS)


def _resident(shape):
    return pl.BlockSpec(shape, lambda *_: (0,) * len(shape), pipeline_mode=pl.Buffered(1))


def _params(*sem):
    return pltpu.CompilerParams(dimension_semantics=sem, vmem_limit_bytes=VMEM_LIMIT)


def _da_inproj_kernel(x_ref, g_ref, wq_ref, wkt_ref, wv_ref, c_ref, s1_ref, s2_ref, ct_ref, st_ref, *rest,
                      n_prev):
    if n_prev:
        ktf_prev_ref, vf_prev_ref, qd_ref, ktf_all_ref, ktb_ref, vf_all_ref, vb_ref = rest
        ktf_all_ref[:n_prev] = ktf_prev_ref[...]
        vf_all_ref[:n_prev] = vf_prev_ref[...]
    else:
        qd_ref, ktf_all_ref, ktb_ref, vf_all_ref, vb_ref = rest
    ktf_ref, vf_ref = ktf_all_ref.at[n_prev], vf_all_ref.at[n_prev]
    hn = (_rms(x_ref[...]) * g_ref[...]).astype(BF16)
    lane = lax.broadcasted_iota(jnp.int32, (1, LANES), 1)
    comp0 = (lane < DA_DK).astype(F32)
    comp1 = 1.0 - comp0
    q = jnp.dot(hn, wq_ref[...], preferred_element_type=F32)
    c, s1, s2 = c_ref[...], s1_ref[...], s2_ref[...]
    for h in range(DA_HEADS):
        sl = slice(h * LANES, (h + 1) * LANES)
        qh = q[:, sl]
        qr = qh * c + pltpu.roll(qh, ROPE_DIMS // 2, 1) * s1 + pltpu.roll(qh, LANES - ROPE_DIMS // 2, 1) * s2
        qd_ref[0, :, sl] = (qr * comp0).astype(BF16)
        qd_ref[1, :, sl] = (qr * comp1).astype(BF16)
    kt = lax.dot_general(wkt_ref[...], hn, _NT, preferred_element_type=F32)
    ct, st = ct_ref[...], st_ref[...]
    half = ROPE_DIMS // 2
    for grp in range(2 * DA_HEADS):
        b0 = grp * DA_DK
        x1 = kt[b0:b0 + half]
        x2 = kt[b0 + half:b0 + 2 * half]
        tail = kt[b0 + 2 * half:b0 + DA_DK]
        o1 = x1 * ct - x2 * st
        o2 = x2 * ct + x1 * st
        for lo, val in ((b0, o1), (b0 + half, o2), (b0 + 2 * half, tail)):
            ktf_ref[lo:lo + val.shape[0], :] = val
            ktb_ref[lo:lo + val.shape[0], :] = val.astype(BF16)
    v = jnp.dot(hn, wv_ref[...], preferred_element_type=F32)
    vb_ref[...] = v.astype(BF16)
    for h in range(DA_HEADS):
        vf_ref[:, h, :] = v[:, h * DA_DV:(h + 1) * DA_DV]


def _drop_refs(body, start, count):
    def wrapped(*refs):
        return body(*refs[:start], *refs[start + count:])
    return wrapped


def _da_inproj(x, g, wq, wkt, wv, tabs, n_valid, tm, prev):
    B, F, _ = x.shape
    c, s1, s2, ct, st = tabs
    n_prev = 0 if prev is None else prev[0].shape[0]
    tok = lambda b, i: (b, i, 0)
    kt_spec = lambda n: pl.BlockSpec((n, None, D_MODEL, tm), lambda b, i: (0, b, 0, i))
    v_spec = lambda n: pl.BlockSpec((n, None, tm, DA_HEADS, DA_DV), lambda b, i: (0, b, i, 0, 0))
    in_specs = [
        pl.BlockSpec((None, tm, D_MODEL), tok),
        _resident((1, D_MODEL)),
        _resident((D_MODEL, D_MODEL)),
        _resident((D_MODEL, D_MODEL)),
        _resident((D_MODEL, D_MODEL)),
        pl.BlockSpec((tm, LANES), lambda b, i: (i, 0)),
        pl.BlockSpec((tm, LANES), lambda b, i: (i, 0)),
        pl.BlockSpec((tm, LANES), lambda b, i: (i, 0)),
        pl.BlockSpec((ROPE_DIMS // 2, tm), lambda b, i: (0, i)),
        pl.BlockSpec((ROPE_DIMS // 2, tm), lambda b, i: (0, i)),
    ]
    if n_prev:
        in_specs += [kt_spec(n_prev), v_spec(n_prev)]
    qd, ktf, ktb, vf, vb = pl.pallas_call(
        functools.partial(_da_inproj_kernel, n_prev=n_prev),
        grid=(B, F // tm),
        in_specs=in_specs,
        out_specs=[
            pl.BlockSpec((2, None, tm, D_MODEL), lambda b, i: (0, b, i, 0)),
            kt_spec(n_prev + 1),
            pl.BlockSpec((None, D_MODEL, tm), lambda b, i: (b, 0, i)),
            v_spec(n_prev + 1),
            pl.BlockSpec((None, tm, D_MODEL), tok),
        ],
        out_shape=[
            jax.ShapeDtypeStruct((2, B, F, D_MODEL), BF16),
            jax.ShapeDtypeStruct((n_prev + 1, B, D_MODEL, n_valid), F32),
            jax.ShapeDtypeStruct((B, D_MODEL, F), BF16),
            jax.ShapeDtypeStruct((n_prev + 1, B, n_valid, DA_HEADS, DA_DV), F32),
            jax.ShapeDtypeStruct((B, F, D_MODEL), BF16),
        ],
        compiler_params=_params("parallel", "parallel"),
        name="da_inproj",
    )(x, g, wq, wkt, wv, c, s1, s2, ct, st, *(prev or ()))
    return qd, ktb, vb, (ktf, vf)


def _da_rope_tables(pos):
    half = ROPE_DIMS // 2
    inv = ROPE_THETA ** (-jnp.arange(0, ROPE_DIMS, 2, dtype=F32) / ROPE_DIMS)
    ang = pos.astype(F32)[:, None] * inv[None, :]
    cos, sin = jnp.cos(ang), jnp.sin(ang)
    n = pos.shape[0]
    zeros = lambda w: jnp.zeros((n, w), F32)
    c64 = jnp.concatenate([cos, cos, jnp.ones((n, DA_DK - ROPE_DIMS), F32)], axis=1)
    s1_64 = jnp.concatenate([zeros(half), sin, zeros(DA_DK - ROPE_DIMS)], axis=1)
    s2_64 = jnp.concatenate([-sin, zeros(DA_DK - half)], axis=1)
    scale = DA_DK ** -0.5 * math.log2(math.e)
    tile2 = lambda a: jnp.concatenate([a, a], axis=1) * scale
    return tile2(c64), tile2(s1_64), tile2(s2_64), cos.T, sin.T


def _da_lambda(lq1_ref, lk1_ref, lq2_ref, lk2_ref, lam_init):
    a = jnp.sum(lq1_ref[...] * lk1_ref[...], axis=-1, keepdims=True)
    b = jnp.sum(lq2_ref[...] * lk2_ref[...], axis=-1, keepdims=True)
    return jnp.exp(a) - jnp.exp(b) + lam_init


def _da_finish(acc0, l0, acc1, l1, lam, g, lam_init):
    o = acc0 / l0 - lam * (acc1 / l1)
    return _rms(o) * g * (1.0 - lam_init)


def _da_prompt_body(i, last, q_ref, kt_ref, v_ref, lq1_ref, lk1_ref, lq2_ref, lk2_ref, g_ref, o_ref,
                    m_sc, l_sc, acc_sc, *, tq, tk, td, rows_last, lam_init):
    m_sc[...] = jnp.full(m_sc.shape, MASK_VALUE, F32)
    l_sc[...] = jnp.zeros(l_sc.shape, F32)
    acc_sc[...] = jnp.zeros(acc_sc.shape, F32)
    n_full = i * (tq // tk)

    def run(rows):
        def step(off, r0, width, masked):
            kt = kt_ref[:, pl.ds(off, width)]
            v = v_ref[pl.ds(off, width), :]
            for c in range(2):
                s = jnp.dot(q_ref[c, r0:rows, :], kt, preferred_element_type=F32)
                if masked:
                    qpos = i * tq + r0 + lax.broadcasted_iota(jnp.int32, s.shape, 0)
                    kpos = off + lax.broadcasted_iota(jnp.int32, s.shape, 1)
                    s = jnp.where(kpos <= qpos, s, MASK_VALUE)
                tiles = [s[:, t * LANES:(t + 1) * LANES] for t in range(width // LANES)]
                m_old = m_sc[c, r0:rows, :]
                blk_max = jnp.max(functools.reduce(jnp.maximum, tiles), axis=-1, keepdims=True)
                m_new = jnp.maximum(m_old, blk_max)
                alpha = jnp.exp2(m_old - m_new)
                ps = [jnp.exp2(t - m_new) for t in tiles]
                l_sc[c, r0:rows, :] = alpha * l_sc[c, r0:rows, :] + functools.reduce(jnp.add, ps)
                p = jnp.concatenate(ps, axis=1).astype(BF16)
                acc_sc[c, r0:rows, :] = alpha * acc_sc[c, r0:rows, :] + jnp.dot(p, v, preferred_element_type=F32)
                m_sc[c, r0:rows, :] = m_new

        def pair_body(j, carry):
            step(pl.multiple_of(2 * j * tk, tk), 0, 2 * tk, False)
            return carry

        lax.fori_loop(0, n_full // 2, pair_body, 0)

        @pl.when(n_full % 2 == 1)
        def _():
            step(pl.multiple_of((n_full - 1) * tk, tk), 0, tk, False)

        for d in range(-(-rows // td)):
            step(pl.multiple_of(i * tq + d * td, td), d * td, td, True)

        lam = _da_lambda(lq1_ref, lk1_ref, lq2_ref, lk2_ref, lam_init)
        l0 = jnp.sum(l_sc[0, :rows, :], axis=-1, keepdims=True)
        l1 = jnp.sum(l_sc[1, :rows, :], axis=-1, keepdims=True)
        o = _da_finish(acc_sc[0, :rows, :], l0, acc_sc[1, :rows, :], l1, lam, g_ref[...], lam_init)
        o_ref[:rows, :] = o.astype(o_ref.dtype)
        if rows < tq:
            o_ref[rows:, :] = jnp.zeros((tq - rows, LANES), o_ref.dtype)

    if rows_last == tq:
        run(tq)
    else:
        pl.when(i < last)(lambda: run(tq))
        pl.when(i == last)(lambda: run(rows_last))


def _da_prompt_kernel(*refs, **static):
    _da_prompt_body(pl.program_id(2), pl.num_programs(2) - 1, *refs, **static)


def _da_prompt(qd, ktb, vb, lams, g, lam_init, n_valid, tq, tk, td):
    _, B, F, _ = qd.shape
    rows_last = _round_up(n_valid - (F // tq - 1) * tq, 16)
    lam_spec = pl.BlockSpec((1, DA_DK), lambda b, h, i: (0, 0))
    return pl.pallas_call(
        functools.partial(_da_prompt_kernel, tq=tq, tk=tk, td=td, rows_last=rows_last, lam_init=lam_init),
        grid=(B, DA_HEADS, F // tq),
        in_specs=[
            pl.BlockSpec((2, None, tq, LANES), lambda b, h, i: (0, b, i, h)),
            pl.BlockSpec((None, LANES, F), lambda b, h, i: (b, h, 0)),
            pl.BlockSpec((None, F, LANES), lambda b, h, i: (b, 0, h)),
            lam_spec, lam_spec, lam_spec, lam_spec,
            pl.BlockSpec((1, DA_DV), lambda b, h, i: (0, 0)),
        ],
        out_specs=pl.BlockSpec((None, tq, LANES), lambda b, h, i: (b, i, h)),
        out_shape=jax.ShapeDtypeStruct((B, F, D_MODEL), BF16),
        scratch_shapes=[
            pltpu.VMEM((2, tq, LANES), F32),
            pltpu.VMEM((2, tq, LANES), F32),
            pltpu.VMEM((2, tq, LANES), F32),
        ],
        compiler_params=_params("parallel", "parallel", "parallel"),
        name="da_prompt_attn",
    )(qd, ktb, vb, *lams, g)


def _da_sample_kernel(pt_ref, q_ref, kn_ref, vn_ref, lq1_ref, lk1_ref, lq2_ref, lk2_ref, g_ref, *rest,
                      n_pages, n_new, lam_init):
    del pt_ref
    _da_sample_body(q_ref, kn_ref, vn_ref, lq1_ref, lk1_ref, lq2_ref, lk2_ref, g_ref,
                    rest[:n_pages], rest[n_pages:2 * n_pages], rest[2 * n_pages], n_new=n_new, lam_init=lam_init)


def _da_sample_body(q_ref, kn_ref, vn_ref, lq1_ref, lk1_ref, lq2_ref, lk2_ref, g_ref, kc_refs, vc_refs, o_ref,
                    *, n_new, lam_init):
    page = vc_refs[0].shape[0] // DA_HEADS
    lam = _da_lambda(lq1_ref, lk1_ref, lq2_ref, lk2_ref, lam_init)
    q = q_ref[...]
    row = lax.broadcasted_iota(jnp.int32, (2 * n_new, 1), 0)
    step = jnp.where(row >= n_new, row - n_new, row)
    for h in range(DA_HEADS):
        sl = slice(h * LANES, (h + 1) * LANES)
        qh = q[:, sl]
        kt = jnp.concatenate([r[h].astype(BF16) for r in kc_refs], axis=1)
        s = jnp.dot(qh, kt, preferred_element_type=F32)
        tiles = [s[:, t * LANES:(t + 1) * LANES] for t in range(s.shape[1] // LANES)]
        qf = qh.astype(F32)
        kn = kn_ref[:, sl]
        s_new = [jnp.where(step >= j, jnp.sum(qf * kn[j:j + 1, :], axis=-1, keepdims=True), MASK_VALUE)
                 for j in range(n_new)]
        m = jnp.max(functools.reduce(jnp.maximum, tiles), axis=-1, keepdims=True)
        m = functools.reduce(jnp.maximum, s_new, m)
        ps = [jnp.exp2(t - m) for t in tiles]
        l = jnp.sum(functools.reduce(jnp.add, ps), axis=-1, keepdims=True)
        v = jnp.concatenate([r[pl.ds(h, page, stride=DA_HEADS), :].astype(BF16) for r in vc_refs], axis=0)
        acc = jnp.dot(jnp.concatenate(ps, axis=1).astype(BF16), v, preferred_element_type=F32)
        for j in range(n_new):
            pj = jnp.exp2(s_new[j] - m)
            l = l + pj
            acc = acc + pj * vn_ref[j:j + 1, h, :]
        o_ref[:, sl] = _da_finish(acc[:n_new], l[:n_new], acc[n_new:], l[n_new:], lam, g_ref[...], lam_init)


def _da_sample(page_table, layer, q2, kc, vc, kn, vn, lams, g, lam_init):
    DB, n_pages = page_table.shape
    T = kn.shape[1]
    page = kc.shape[-1]
    lam_spec = pl.BlockSpec((1, DA_DK), lambda b, pt: (0, 0))
    k_spec = lambda r: pl.BlockSpec((None, None, DA_HEADS, LANES, page), lambda b, pt: (layer, pt[b, r], 0, 0, 0))
    v_spec = lambda r: pl.BlockSpec((None, None, page * DA_HEADS, DA_DV), lambda b, pt: (layer, pt[b, r], 0, 0))
    grid_spec = pltpu.PrefetchScalarGridSpec(
        num_scalar_prefetch=1,
        grid=(DB,),
        in_specs=[
            pl.BlockSpec((None, 2 * T, D_MODEL), lambda b, pt: (b, 0, 0)),
            pl.BlockSpec((None, T, D_MODEL), lambda b, pt: (b, 0, 0)),
            pl.BlockSpec((None, T, DA_HEADS, DA_DV), lambda b, pt: (b, 0, 0, 0)),
            lam_spec, lam_spec, lam_spec, lam_spec,
            pl.BlockSpec((1, DA_DV), lambda b, pt: (0, 0)),
        ] + [k_spec(r) for r in range(n_pages)] + [v_spec(r) for r in range(n_pages)],
        out_specs=pl.BlockSpec((None, T, D_MODEL), lambda b, pt: (b, 0, 0)),
    )
    return pl.pallas_call(
        functools.partial(_da_sample_kernel, n_pages=n_pages, n_new=T, lam_init=lam_init),
        grid_spec=grid_spec,
        out_shape=jax.ShapeDtypeStruct((DB, T, D_MODEL), F32),
        compiler_params=_params("parallel"),
        name="da_sample_attn",
    )(page_table, q2, kn, vn, *lams, g, *([kc] * n_pages), *([vc] * n_pages))


def _da_fused_kernel(pt_ref, q_ref, kt_ref, v_ref, lq1_ref, lk1_ref, lq2_ref, lk2_ref, g_ref,
                     q2_ref, kn_ref, vn_ref, *rest, n_pages, n_seq, n_new, **prompt_static):
    del pt_ref
    kc_refs, vc_refs = rest[:n_pages], rest[n_pages:2 * n_pages]
    o_ref, os_ref, m_sc, l_sc, acc_sc = rest[2 * n_pages:]
    lams = (lq1_ref, lk1_ref, lq2_ref, lk2_ref)
    i, nq = pl.program_id(2), pl.num_programs(2)
    _da_prompt_body(i, nq - 1, q_ref, kt_ref, v_ref, *lams, g_ref, o_ref, m_sc, l_sc, acc_sc, **prompt_static)
    step = (pl.program_id(0) * pl.num_programs(1) + pl.program_id(1)) * nq + i

    @pl.when(step < n_seq)
    def _():
        _da_sample_body(q2_ref, kn_ref, vn_ref, *lams, g_ref, kc_refs, vc_refs, os_ref,
                        n_new=n_new, lam_init=prompt_static["lam_init"])


def _da_fused(page_table, layer, qd, ktb, vb, q2, kc, vc, kn, vn, lams, g, lam_init, n_valid, tq, tk, td):
    _, B, F, _ = qd.shape
    DB, n_pages = page_table.shape
    T = kn.shape[1]
    page = kc.shape[-1]
    nq = F // tq
    rows_last = _round_up(n_valid - (nq - 1) * tq, 16)
    seq = lambda b, h, i: jnp.minimum((b * DA_HEADS + h) * nq + i, DB - 1)
    lam_spec = pl.BlockSpec((1, DA_DK), lambda b, h, i, pt: (0, 0))
    k_spec = lambda r: pl.BlockSpec((None, None, DA_HEADS, LANES, page),
                                    lambda b, h, i, pt: (layer, pt[seq(b, h, i), r], 0, 0, 0))
    v_spec = lambda r: pl.BlockSpec((None, None, page * DA_HEADS, DA_DV),
                                    lambda b, h, i, pt: (layer, pt[seq(b, h, i), r], 0, 0))
    grid_spec = pltpu.PrefetchScalarGridSpec(
        num_scalar_prefetch=1,
        grid=(B, DA_HEADS, nq),
        in_specs=[
            pl.BlockSpec((2, None, tq, LANES), lambda b, h, i, pt: (0, b, i, h)),
            pl.BlockSpec((None, LANES, F), lambda b, h, i, pt: (b, h, 0)),
            pl.BlockSpec((None, F, LANES), lambda b, h, i, pt: (b, 0, h)),
            lam_spec, lam_spec, lam_spec, lam_spec,
            pl.BlockSpec((1, DA_DV), lambda b, h, i, pt: (0, 0)),
            pl.BlockSpec((None, 2 * T, D_MODEL), lambda b, h, i, pt: (seq(b, h, i), 0, 0)),
            pl.BlockSpec((None, T, D_MODEL), lambda b, h, i, pt: (seq(b, h, i), 0, 0)),
            pl.BlockSpec((None, T, DA_HEADS, DA_DV), lambda b, h, i, pt: (seq(b, h, i), 0, 0, 0)),
        ] + [k_spec(r) for r in range(n_pages)] + [v_spec(r) for r in range(n_pages)],
        out_specs=[
            pl.BlockSpec((None, tq, LANES), lambda b, h, i, pt: (b, i, h)),
            pl.BlockSpec((None, T, D_MODEL), lambda b, h, i, pt: (seq(b, h, i), 0, 0)),
        ],
        scratch_shapes=[
            pltpu.VMEM((2, tq, LANES), F32),
            pltpu.VMEM((2, tq, LANES), F32),
            pltpu.VMEM((2, tq, LANES), F32),
        ],
    )
    return pl.pallas_call(
        functools.partial(_da_fused_kernel, n_pages=n_pages, n_seq=DB, n_new=T, tq=tq, tk=tk, td=td,
                          rows_last=rows_last, lam_init=lam_init),
        grid_spec=grid_spec,
        out_shape=[
            jax.ShapeDtypeStruct((B, F, D_MODEL), BF16),
            jax.ShapeDtypeStruct((DB, T, D_MODEL), F32),
        ],
        compiler_params=_params("arbitrary", "arbitrary", "arbitrary"),
        name="da_attn_fused",
    )(page_table, qd, ktb, vb, *lams, g, q2, kn, vn, *([kc] * n_pages), *([vc] * n_pages))


def _ffn_kernel(x_ref, o_ref, wo_ref, g_ref, win_ref, wout_ref, gout_ref, y_ref, acc_sc, *, final_norm):
    h = x_ref[...] + jnp.dot(o_ref[...].astype(BF16), wo_ref[...], preferred_element_type=F32)
    hn = (_rms(h) * g_ref[...]).astype(BF16)
    acc_sc[...] = h
    for j in range(FFN_HIDDEN // FFN_CHUNK):
        lo = j * FFN_CHUNK
        gate = jnp.dot(hn, win_ref[:, lo:lo + FFN_CHUNK], preferred_element_type=F32)
        up = jnp.dot(hn, win_ref[:, FFN_HIDDEN + lo:FFN_HIDDEN + lo + FFN_CHUNK], preferred_element_type=F32)
        act = (gate * jax.nn.sigmoid(gate) * up).astype(BF16)
        acc_sc[...] += jnp.dot(act, wout_ref[lo:lo + FFN_CHUNK, :], preferred_element_type=F32)
    y = acc_sc[...]
    if final_norm:
        y = _rms(y) * gout_ref[...]
    y_ref[...] = y


def _outproj_ffn(x, o, wo, g, win, wout, gout, tm, final_norm):
    N = x.shape[0]
    ko = o.shape[1]
    return pl.pallas_call(
        functools.partial(_ffn_kernel, final_norm=final_norm),
        grid=(N // tm,),
        in_specs=[
            pl.BlockSpec((tm, D_MODEL), lambda i: (i, 0)),
            pl.BlockSpec((tm, ko), lambda i: (i, 0)),
            _resident((ko, D_MODEL)),
            _resident((1, D_MODEL)),
            _resident((D_MODEL, 2 * FFN_HIDDEN)),
            _resident((FFN_HIDDEN, D_MODEL)),
            _resident((1, D_MODEL)),
        ],
        out_specs=pl.BlockSpec((tm, D_MODEL), lambda i: (i, 0)),
        out_shape=jax.ShapeDtypeStruct((N, D_MODEL), F32),
        scratch_shapes=[pltpu.VMEM((tm, D_MODEL), F32)],
        compiler_params=_params("parallel"),
        name="outproj_ffn",
    )(x, o, wo, g, win, wout, gout)


def _ret_inproj_kernel(x_ref, g_ref, wq_ref, wkt_ref, wv_ref, wg_ref, c_ref, a_ref, b_ref,
                       ct_ref, at_ref, bt_ref, q_ref, kt_ref, v_ref, sg_ref):
    hn = (_rms(x_ref[...]) * g_ref[...]).astype(BF16)
    q = jnp.dot(hn, wq_ref[...], preferred_element_type=F32)
    c, a, b = c_ref[...], a_ref[...], b_ref[...]
    for h in range(RET_HEADS):
        sl = slice(h * RET_DK, (h + 1) * RET_DK)
        qh = q[:, sl]
        qr = qh * c + pltpu.roll(qh, RET_DK - 1, 1) * a + pltpu.roll(qh, 1, 1) * b
        q_ref[:, sl] = qr.astype(BF16)
    kt = lax.dot_general(wkt_ref[...], hn, _NT, preferred_element_type=F32)
    ct, at, bt = ct_ref[...], at_ref[...], bt_ref[...]
    for h in range(RET_HEADS):
        sl = slice(h * RET_DK, (h + 1) * RET_DK)
        kh = kt[sl]
        kr = kh * ct + pltpu.roll(kh, RET_DK - 1, 0) * at + pltpu.roll(kh, 1, 0) * bt
        kt_ref[sl, :] = kr.astype(BF16)
    v_ref[...] = jnp.dot(hn, wv_ref[...], preferred_element_type=F32).astype(BF16)
    gate = jnp.dot(hn, wg_ref[...], preferred_element_type=F32)
    sg_ref[...] = (gate * jax.nn.sigmoid(gate)).astype(BF16)


def _ret_inproj(x, g, wq, wkt, wv, wg, tabs, tm):
    B, F, _ = x.shape
    c, a, b, ct, at, bt = tabs
    tok = lambda bb, i: (bb, i, 0)
    tab = pl.BlockSpec((tm, RET_DK), lambda bb, i: (i, 0))
    tab_t = pl.BlockSpec((RET_DK, tm), lambda bb, i: (0, i))
    return pl.pallas_call(
        _ret_inproj_kernel,
        grid=(B, F // tm),
        in_specs=[
            pl.BlockSpec((None, tm, D_MODEL), tok),
            _resident((1, D_MODEL)),
            _resident((D_MODEL, RET_QK)),
            _resident((RET_QK, D_MODEL)),
            _resident((D_MODEL, RET_VD)),
            _resident((D_MODEL, RET_VD)),
            tab, tab, tab, tab_t, tab_t, tab_t,
        ],
        out_specs=[
            pl.BlockSpec((None, tm, RET_QK), tok),
            pl.BlockSpec((None, RET_QK, tm), lambda bb, i: (bb, 0, i)),
            pl.BlockSpec((None, tm, RET_VD), tok),
            pl.BlockSpec((None, tm, RET_VD), tok),
        ],
        out_shape=[
            jax.ShapeDtypeStruct((B, F, RET_QK), BF16),
            jax.ShapeDtypeStruct((B, RET_QK, F), BF16),
            jax.ShapeDtypeStruct((B, F, RET_VD), BF16),
            jax.ShapeDtypeStruct((B, F, RET_VD), BF16),
        ],
        compiler_params=_params("parallel", "parallel"),
        name="ret_inproj",
    )(x, g, wq, wkt, wv, wg, c, a, b, ct, at, bt)


def _ret_rot_tables(pos):
    angle = 1.0 / (RET_THETA ** jnp.linspace(0.0, 1.0, RET_DK // 2, dtype=F32))
    angle = jnp.repeat(angle, 2)
    ang = pos.astype(F32)[:, None] * angle[None, :]
    cos, sin = jnp.cos(ang), jnp.sin(ang)
    even = (jnp.arange(RET_DK) % 2 == 0)[None, :]
    a = jnp.where(even, -sin, 0.0)
    b = jnp.where(even, 0.0, sin)
    ks = RET_DK ** -0.5
    return cos, a, b, (cos * ks).T, (a * ks).T, (b * ks).T


def _ret_log_decay(h):
    return math.log(1.0 - 2.0 ** (-5.0 - h))


def _ret_gate_out(o, sg):
    return (sg.astype(F32) * _rms(o)).astype(BF16)


def _ret_prompt_kernel(q_ref, kt_ref, v_ref, sg_ref, og_ref, st_ref, s_sc, *, n_valid):
    c_idx = pl.program_id(1)

    @pl.when(c_idx == 0)
    def _():
        s_sc[...] = jnp.zeros(s_sc.shape, F32)

    C = RET_CHUNK
    valid = jnp.clip(n_valid - c_idx * C, 0, C).astype(F32)
    valid_v = jnp.full((1, 1), valid, F32)
    n_row = lax.broadcasted_iota(jnp.int32, (C, 1), 0).astype(F32)
    n_col = lax.broadcasted_iota(jnp.int32, (1, C), 1).astype(F32)
    diff = n_row - n_col
    for h in range(RET_HEADS):
        lg = _ret_log_decay(h)
        q = q_ref[:, h * RET_DK:(h + 1) * RET_DK]
        kt = kt_ref[h * RET_DK:(h + 1) * RET_DK, :]
        v = v_ref[:, h * RET_DV:(h + 1) * RET_DV]
        dmask = jnp.where(diff >= 0, jnp.exp(lg * jnp.maximum(diff, 0.0)), 0.0)
        s = jnp.dot(q, kt, preferred_element_type=F32) * dmask
        inner = jnp.dot(s.astype(BF16), v, preferred_element_type=F32)
        state = s_sc[h]
        q_dec = (q.astype(F32) * jnp.exp(lg * (n_row + 1.0))).astype(BF16)
        cross = jnp.dot(q_dec, state.astype(BF16), preferred_element_type=F32)
        k_decay = jnp.where(n_col < valid, jnp.exp(lg * (valid - 1.0 - n_col)), 0.0)
        k_dec = (kt.astype(F32) * k_decay).astype(BF16)
        s_sc[h] = jnp.exp(lg * valid_v) * state + jnp.dot(k_dec, v, preferred_element_type=F32)
        sl = slice(h * RET_DV, (h + 1) * RET_DV)
        og_ref[:, sl] = _ret_gate_out(inner + cross, sg_ref[:, sl])

    @pl.when(c_idx == pl.num_programs(1) - 1)
    def _():
        st_ref[...] = s_sc[...]


def _ret_prompt(q, kt, v, sg, n_valid):
    B, F, _ = q.shape
    C = RET_CHUNK
    tok = lambda b, c: (b, c, 0)
    return pl.pallas_call(
        functools.partial(_ret_prompt_kernel, n_valid=n_valid),
        grid=(B, F // C),
        in_specs=[
            pl.BlockSpec((None, C, RET_QK), tok),
            pl.BlockSpec((None, RET_QK, C), lambda b, c: (b, 0, c)),
            pl.BlockSpec((None, C, RET_VD), tok),
            pl.BlockSpec((None, C, RET_VD), tok),
        ],
        out_specs=[
            pl.BlockSpec((None, C, RET_VD), tok),
            pl.BlockSpec((None, RET_HEADS, RET_DK, RET_DV), lambda b, c: (b, 0, 0, 0)),
        ],
        out_shape=[
            jax.ShapeDtypeStruct((B, F, RET_VD), BF16),
            jax.ShapeDtypeStruct((B, RET_HEADS, RET_DK, RET_DV), F32),
        ],
        scratch_shapes=[pltpu.VMEM((RET_HEADS, RET_DK, RET_DV), F32)],
        compiler_params=_params("parallel", "arbitrary"),
        name="ret_prompt",
    )(q, kt, v, sg)


def _ret_sample_kernel(q_ref, k_ref, v_ref, sg_ref, st_in_ref, og_ref, st_out_ref, *, n_new):
    T = n_new
    n_row = lax.broadcasted_iota(jnp.int32, (T, 1), 0).astype(F32)
    n_col = lax.broadcasted_iota(jnp.int32, (1, T), 1).astype(F32)
    diff = n_row - n_col
    for h in range(RET_HEADS):
        lg = _ret_log_decay(h)
        q = q_ref[:, h * RET_DK:(h + 1) * RET_DK].astype(F32)
        k = k_ref[:, h * RET_DK:(h + 1) * RET_DK].astype(F32)
        v = v_ref[:, h * RET_DV:(h + 1) * RET_DV].astype(F32)
        dmask = jnp.where(diff >= 0, jnp.exp(lg * jnp.maximum(diff, 0.0)), 0.0)
        s = lax.dot_general(q, k, _NT, preferred_element_type=F32) * dmask
        inner = jnp.dot(s, v, preferred_element_type=F32)
        state = st_in_ref[h]
        q_dec = (q * jnp.exp(lg * (n_row + 1.0))).astype(BF16)
        cross = jnp.dot(q_dec, state.astype(BF16), preferred_element_type=F32)
        k_dec = k * jnp.exp(lg * (T - 1.0 - n_row))
        st_out_ref[h] = math.exp(lg * T) * state + lax.dot_general(k_dec, v, _TN, preferred_element_type=F32)
        sl = slice(h * RET_DV, (h + 1) * RET_DV)
        og_ref[:, sl] = _ret_gate_out(inner + cross, sg_ref[:, sl])


def _ret_sample(q, k, v, sg, state_all, layer, prev):
    DB, T, _ = q.shape
    tok = lambda b: (b, 0, 0)
    st = pl.BlockSpec((None, None, RET_HEADS, RET_DK, RET_DV), lambda b: (layer, b, 0, 0, 0))
    n_in = 5
    in_specs = [
        pl.BlockSpec((None, T, RET_QK), tok),
        pl.BlockSpec((None, T, RET_QK), tok),
        pl.BlockSpec((None, T, RET_VD), tok),
        pl.BlockSpec((None, T, RET_VD), tok),
        st,
    ]
    body = functools.partial(_ret_sample_kernel, n_new=T)
    aliases, extra = {}, ()
    if prev is not None:
        body = _drop_refs(body, n_in, 1)
        in_specs.append(pl.BlockSpec(memory_space=pl.ANY))
        aliases = {n_in: 1}
        extra = (prev,)
    return pl.pallas_call(
        body,
        grid=(DB,),
        in_specs=in_specs,
        out_specs=[pl.BlockSpec((None, T, RET_VD), tok), st],
        out_shape=[
            jax.ShapeDtypeStruct((DB, T, RET_VD), BF16),
            jax.ShapeDtypeStruct(state_all.shape, F32),
        ],
        input_output_aliases=aliases,
        compiler_params=_params("parallel"),
        name="ret_sample",
    )(q, k, v, sg, state_all, *extra)


def kernel(x_prompt, x_sample, cache_k, cache_v, state_ret, page_table, meta_tokens, norm_mix, norm_ffn, norm_out, da_w_in, da_w_out, da_lambda_q1, da_lambda_k1, da_lambda_q2, da_lambda_k2, da_subln, ret_w_in, ret_w_out, ffn_w_in, ffn_w_out):
    B, seq, _ = x_prompt.shape
    DB, T, _ = x_sample.shape
    depth = norm_mix.shape[0]
    L = N_META + seq
    NS = DB * T
    page = cache_k.shape[2]
    past = page_table.shape[1] * page

    tp = _tiles(L)
    ts = {k: min(t, NS) for k, t in _tiles(NS).items()}
    F = _round_up(L, tp["frame"])

    meta = jnp.broadcast_to(meta_tokens.astype(F32)[None], (B, N_META, D_MODEL))
    hp = jnp.concatenate([meta, x_prompt, jnp.zeros((B, F - L, D_MODEL), F32)], axis=1)
    hs = x_sample.reshape(1, NS, D_MODEL)

    pos_p = jnp.arange(F)
    pos_s = past + jnp.arange(NS) % T
    da_tabs_p, da_tabs_s = _da_rope_tables(pos_p), _da_rope_tables(pos_s)
    ret_tabs_p, ret_tabs_s = _ret_rot_tables(pos_p), _ret_rot_tables(pos_s)

    kc = jnp.transpose(cache_k, (0, 1, 3, 4, 5, 2)).reshape(cache_k.shape[0], cache_k.shape[1], DA_HEADS, 2 * DA_DK, page)
    vc = cache_v.reshape(cache_v.shape[0], cache_v.shape[1], page * DA_HEADS, DA_DV)

    row = lambda a: a.astype(F32).reshape(1, -1)
    st_s_all = None
    kv_p = kv_s = None
    st_p_list = []
    for i in range(depth):
        j = i // N_MIXERS
        g_mix = row(norm_mix[i])
        if i % N_MIXERS == 0:
            lam_init = 0.8 - 0.6 * math.exp(-0.3 * i)
            w = da_w_in[j]
            wq = w[:, :D_MODEL].astype(BF16)
            wkt = w[:, D_MODEL:2 * D_MODEL].T.astype(BF16)
            wv = w[:, 2 * D_MODEL:].astype(BF16)
            lams = (row(da_lambda_q1[j]), row(da_lambda_k1[j]), row(da_lambda_q2[j]), row(da_lambda_k2[j]))
            g_sub = row(da_subln[j])
            qd, ktb, vb, kv_p = _da_inproj(hp, g_mix, wq, wkt, wv, da_tabs_p, L, tp["tm_proj"], kv_p)
            qd_s, _, _, kv_s = _da_inproj(hs, g_mix, wq, wkt, wv, da_tabs_s, NS, ts["tm_proj"], kv_s)
            q2 = qd_s.reshape(2, DB, T, D_MODEL).transpose(1, 0, 2, 3).reshape(DB, 2 * T, D_MODEL)
            kn = kv_s[0][-1, 0].T.reshape(DB, T, D_MODEL)
            vn = kv_s[1][-1].reshape(DB, T, DA_HEADS, DA_DV)
            if DB <= B * DA_HEADS * (F // tp["tq"]):
                op, osm = _da_fused(page_table, j, qd, ktb, vb, q2, kc, vc, kn, vn, lams, g_sub, lam_init,
                                    L, tp["tq"], tp["tk"], tp["td"])
            else:
                op = _da_prompt(qd, ktb, vb, lams, g_sub, lam_init, L, tp["tq"], tp["tk"], tp["td"])
                osm = _da_sample(page_table, j, q2, kc, vc, kn, vn, lams, g_sub, lam_init)
            o_p, o_s = op.reshape(B * F, D_MODEL), osm.reshape(NS, D_MODEL)
            wo = da_w_out[j].astype(BF16)
        else:
            w = ret_w_in[j]
            wq = w[:, :RET_QK].astype(BF16)
            wkt = w[:, RET_QK:2 * RET_QK].T.astype(BF16)
            wv = w[:, 2 * RET_QK:2 * RET_QK + RET_VD].astype(BF16)
            wg = w[:, 2 * RET_QK + RET_VD:].astype(BF16)
            q, kt, v, sg = _ret_inproj(hp, g_mix, wq, wkt, wv, wg, ret_tabs_p, tp["tm_proj"])
            q_s, kt_s, v_s, sg_s = _ret_inproj(hs, g_mix, wq, wkt, wv, wg, ret_tabs_s, ts["tm_proj"])
            og, st_p = _ret_prompt(q, kt, v, sg, L)
            per_seq = lambda a: a.reshape(DB, T, a.shape[-1])
            og_s, st_s_all = _ret_sample(per_seq(q_s[0]), per_seq(kt_s[0].T), per_seq(v_s[0]), per_seq(sg_s[0]),
                                         state_ret, j, st_s_all)
            o_p, o_s = og.reshape(B * F, RET_VD), og_s.reshape(NS, RET_VD)
            wo = ret_w_out[j].astype(BF16)
            st_p_list.append(st_p)
        last = i == depth - 1
        ffn = (wo, row(norm_ffn[i]), ffn_w_in[i].astype(BF16), ffn_w_out[i].astype(BF16), row(norm_out))
        hp = _outproj_ffn(hp.reshape(B * F, D_MODEL), o_p, *ffn, tp["tm_ffn"], last).reshape(B, F, D_MODEL)
        hs = _outproj_ffn(hs.reshape(NS, D_MODEL), o_s, *ffn, ts["tm_ffn"], last).reshape(1, NS, D_MODEL)
    y_prompt = hp[:, N_META:L]
    y_sample = hs.reshape(DB, T, D_MODEL)
    n_att = kv_p[0].shape[0]
    new_k_prompt = kv_p[0].reshape(n_att, B, DA_HEADS, 2, DA_DK, L).transpose(0, 1, 5, 2, 3, 4)
    new_k_sample = kv_s[0].reshape(n_att, DA_HEADS, 2, DA_DK, DB, T).transpose(0, 4, 5, 1, 2, 3)
    new_v_sample = kv_s[1].reshape(n_att, DB, T, DA_HEADS, DA_DV)
    return (y_prompt, y_sample, new_k_prompt, kv_p[1], new_k_sample, new_v_sample,
            jnp.stack(st_p_list), st_s_all)
```

```python
import functools
import math

import jax
import jax.numpy as jnp
from jax import lax
from jax.experimental import pallas as pl
from jax.experimental.pallas import tpu as pltpu

F32, BF16 = jnp.float32, jnp.bfloat16

D_MODEL = 1024
N_META = 16
N_MIXERS = 2
DA_HEADS = 8
DA_DK = 64
DA_DV = 128
ROPE_THETA = 500000.0
ROPE_DIMS = 16
RET_HEADS = 4
RET_DK = 256
RET_DV = 512
RET_QK = RET_HEADS * RET_DK
RET_VD = RET_HEADS * RET_DV
RET_CHUNK = 256
RET_THETA = 10000.0
FFN_HIDDEN = 2816
FFN_CHUNK = 256
RMS_EPS = 1e-6
MASK_VALUE = -1e30
LANES = 128
VMEM_LIMIT = 60 * 1024 * 1024

_NT = (((1,), (1,)), ((), ()))
_TN = (((0,), (0,)), ((), ()))


def _tiles(n_rows):
    if n_rows >= 4096:
        return dict(frame=768, tm_proj=384, tm_ffn=768, tq=768, tk=768, td=256)
    if n_rows >= 512:
        return dict(frame=512, tm_proj=512, tm_ffn=512, tq=512, tk=256, td=256)
    return dict(frame=256, tm_proj=128, tm_ffn=128, tq=256, tk=128, td=128)


def _round_up(x, m):
    return -(-x // m) * m


def _rms(x):
    return x * lax.rsqrt(jnp.mean(x * x, axis=-1, keepdims=True) + RMS_EPS)


def _resident(shape):
    return pl.BlockSpec(shape, lambda *_: (0,) * len(shape), pipeline_mode=pl.Buffered(1))


def _params(*sem):
    return pltpu.CompilerParams(dimension_semantics=sem, vmem_limit_bytes=VMEM_LIMIT)


def _da_inproj_kernel(x_ref, g_ref, wq_ref, wkt_ref, wv_ref, c_ref, s1_ref, s2_ref, ct_ref, st_ref, *rest,
                      n_prev):
    if n_prev:
        ktf_prev_ref, vf_prev_ref, qd_ref, ktf_all_ref, ktb_ref, vf_all_ref, vb_ref = rest
        ktf_all_ref[:n_prev] = ktf_prev_ref[...]
        vf_all_ref[:n_prev] = vf_prev_ref[...]
    else:
        qd_ref, ktf_all_ref, ktb_ref, vf_all_ref, vb_ref = rest
    ktf_ref, vf_ref = ktf_all_ref.at[n_prev], vf_all_ref.at[n_prev]
    hn = (_rms(x_ref[...]) * g_ref[...]).astype(BF16)
    lane = lax.broadcasted_iota(jnp.int32, (1, LANES), 1)
    comp0 = (lane < DA_DK).astype(F32)
    comp1 = 1.0 - comp0
    q = jnp.dot(hn, wq_ref[...], preferred_element_type=F32)
    c, s1, s2 = c_ref[...], s1_ref[...], s2_ref[...]
    for h in range(DA_HEADS):
        sl = slice(h * LANES, (h + 1) * LANES)
        qh = q[:, sl]
        qr = qh * c + pltpu.roll(qh, ROPE_DIMS // 2, 1) * s1 + pltpu.roll(qh, LANES - ROPE_DIMS // 2, 1) * s2
        qd_ref[0, :, sl] = (qr * comp0).astype(BF16)
        qd_ref[1, :, sl] = (qr * comp1).astype(BF16)
    kt = lax.dot_general(wkt_ref[...], hn, _NT, preferred_element_type=F32)
    ct, st = ct_ref[...], st_ref[...]
    half = ROPE_DIMS // 2
    for grp in range(2 * DA_HEADS):
        b0 = grp * DA_DK
        x1 = kt[b0:b0 + half]
        x2 = kt[b0 + half:b0 + 2 * half]
        tail = kt[b0 + 2 * half:b0 + DA_DK]
        o1 = x1 * ct - x2 * st
        o2 = x2 * ct + x1 * st
        for lo, val in ((b0, o1), (b0 + half, o2), (b0 + 2 * half, tail)):
            ktf_ref[lo:lo + val.shape[0], :] = val
            ktb_ref[lo:lo + val.shape[0], :] = val.astype(BF16)
    v = jnp.dot(hn, wv_ref[...], preferred_element_type=F32)
    vb_ref[...] = v.astype(BF16)
    for h in range(DA_HEADS):
        vf_ref[:, h, :] = v[:, h * DA_DV:(h + 1) * DA_DV]


def _drop_refs(body, start, count):
    def wrapped(*refs):
        return body(*refs[:start], *refs[start + count:])
    return wrapped


def _da_inproj(x, g, wq, wkt, wv, tabs, n_valid, tm, prev):
    B, F, _ = x.shape
    c, s1, s2, ct, st = tabs
    n_prev = 0 if prev is None else prev[0].shape[0]
    tok = lambda b, i: (b, i, 0)
    kt_spec = lambda n: pl.BlockSpec((n, None, D_MODEL, tm), lambda b, i: (0, b, 0, i))
    v_spec = lambda n: pl.BlockSpec((n, None, tm, DA_HEADS, DA_DV), lambda b, i: (0, b, i, 0, 0))
    in_specs = [
        pl.BlockSpec((None, tm, D_MODEL), tok),
        _resident((1, D_MODEL)),
        _resident((D_MODEL, D_MODEL)),
        _resident((D_MODEL, D_MODEL)),
        _resident((D_MODEL, D_MODEL)),
        pl.BlockSpec((tm, LANES), lambda b, i: (i, 0)),
        pl.BlockSpec((tm, LANES), lambda b, i: (i, 0)),
        pl.BlockSpec((tm, LANES), lambda b, i: (i, 0)),
        pl.BlockSpec((ROPE_DIMS // 2, tm), lambda b, i: (0, i)),
        pl.BlockSpec((ROPE_DIMS // 2, tm), lambda b, i: (0, i)),
    ]
    if n_prev:
        in_specs += [kt_spec(n_prev), v_spec(n_prev)]
    qd, ktf, ktb, vf, vb = pl.pallas_call(
        functools.partial(_da_inproj_kernel, n_prev=n_prev),
        grid=(B, F // tm),
        in_specs=in_specs,
        out_specs=[
            pl.BlockSpec((2, None, tm, D_MODEL), lambda b, i: (0, b, i, 0)),
            kt_spec(n_prev + 1),
            pl.BlockSpec((None, D_MODEL, tm), lambda b, i: (b, 0, i)),
            v_spec(n_prev + 1),
            pl.BlockSpec((None, tm, D_MODEL), tok),
        ],
        out_shape=[
            jax.ShapeDtypeStruct((2, B, F, D_MODEL), BF16),
            jax.ShapeDtypeStruct((n_prev + 1, B, D_MODEL, n_valid), F32),
            jax.ShapeDtypeStruct((B, D_MODEL, F), BF16),
            jax.ShapeDtypeStruct((n_prev + 1, B, n_valid, DA_HEADS, DA_DV), F32),
            jax.ShapeDtypeStruct((B, F, D_MODEL), BF16),
        ],
        compiler_params=_params("parallel", "parallel"),
        name="da_inproj",
    )(x, g, wq, wkt, wv, c, s1, s2, ct, st, *(prev or ()))
    return qd, ktb, vb, (ktf, vf)


def _da_rope_tables(pos):
    half = ROPE_DIMS // 2
    inv = ROPE_THETA ** (-jnp.arange(0, ROPE_DIMS, 2, dtype=F32) / ROPE_DIMS)
    ang = pos.astype(F32)[:, None] * inv[None, :]
    cos, sin = jnp.cos(ang), jnp.sin(ang)
    n = pos.shape[0]
    zeros = lambda w: jnp.zeros((n, w), F32)
    c64 = jnp.concatenate([cos, cos, jnp.ones((n, DA_DK - ROPE_DIMS), F32)], axis=1)
    s1_64 = jnp.concatenate([zeros(half), sin, zeros(DA_DK - ROPE_DIMS)], axis=1)
    s2_64 = jnp.concatenate([-sin, zeros(DA_DK - half)], axis=1)
    scale = DA_DK ** -0.5 * math.log2(math.e)
    tile2 = lambda a: jnp.concatenate([a, a], axis=1) * scale
    return tile2(c64), tile2(s1_64), tile2(s2_64), cos.T, sin.T


def _da_lambda(lq1_ref, lk1_ref, lq2_ref, lk2_ref, lam_init):
    a = jnp.sum(lq1_ref[...] * lk1_ref[...], axis=-1, keepdims=True)
    b = jnp.sum(lq2_ref[...] * lk2_ref[...], axis=-1, keepdims=True)
    return jnp.exp(a) - jnp.exp(b) + lam_init


def _da_finish(acc0, l0, acc1, l1, lam, g, lam_init):
    o = acc0 / l0 - lam * (acc1 / l1)
    return _rms(o) * g * (1.0 - lam_init)


def _da_prompt_body(i, last, q_ref, kt_ref, v_ref, lq1_ref, lk1_ref, lq2_ref, lk2_ref, g_ref, o_ref,
                    m_sc, l_sc, acc_sc, *, tq, tk, td, rows_last, lam_init):
    m_sc[...] = jnp.full(m_sc.shape, MASK_VALUE, F32)
    l_sc[...] = jnp.zeros(l_sc.shape, F32)
    acc_sc[...] = jnp.zeros(acc_sc.shape, F32)
    n_full = i * (tq // tk)

    def run(rows):
        def step(off, r0, width, masked):
            kt = kt_ref[:, pl.ds(off, width)]
            v = v_ref[pl.ds(off, width), :]
            for c in range(2):
                s = jnp.dot(q_ref[c, r0:rows, :], kt, preferred_element_type=F32)
                if masked:
                    qpos = i * tq + r0 + lax.broadcasted_iota(jnp.int32, s.shape, 0)
                    kpos = off + lax.broadcasted_iota(jnp.int32, s.shape, 1)
                    s = jnp.where(kpos <= qpos, s, MASK_VALUE)
                tiles = [s[:, t * LANES:(t + 1) * LANES] for t in range(width // LANES)]
                m_old = m_sc[c, r0:rows, :]
                blk_max = jnp.max(functools.reduce(jnp.maximum, tiles), axis=-1, keepdims=True)
                m_new = jnp.maximum(m_old, blk_max)
                alpha = jnp.exp2(m_old - m_new)
                ps = [jnp.exp2(t - m_new) for t in tiles]
                l_sc[c, r0:rows, :] = alpha * l_sc[c, r0:rows, :] + functools.reduce(jnp.add, ps)
                p = jnp.concatenate(ps, axis=1).astype(BF16)
                acc_sc[c, r0:rows, :] = alpha * acc_sc[c, r0:rows, :] + jnp.dot(p, v, preferred_element_type=F32)
                m_sc[c, r0:rows, :] = m_new

        def pair_body(j, carry):
            step(pl.multiple_of(2 * j * tk, tk), 0, 2 * tk, False)
            return carry

        lax.fori_loop(0, n_full // 2, pair_body, 0)

        @pl.when(n_full % 2 == 1)
        def _():
            step(pl.multiple_of((n_full - 1) * tk, tk), 0, tk, False)

        for d in range(-(-rows // td)):
            step(pl.multiple_of(i * tq + d * td, td), d * td, td, True)

        lam = _da_lambda(lq1_ref, lk1_ref, lq2_ref, lk2_ref, lam_init)
        l0 = jnp.sum(l_sc[0, :rows, :], axis=-1, keepdims=True)
        l1 = jnp.sum(l_sc[1, :rows, :], axis=-1, keepdims=True)
        o = _da_finish(acc_sc[0, :rows, :], l0, acc_sc[1, :rows, :], l1, lam, g_ref[...], lam_init)
        o_ref[:rows, :] = o.astype(o_ref.dtype)
        if rows < tq:
            o_ref[rows:, :] = jnp.zeros((tq - rows, LANES), o_ref.dtype)

    if rows_last == tq:
        run(tq)
    else:
        pl.when(i < last)(lambda: run(tq))
        pl.when(i == last)(lambda: run(rows_last))


def _da_prompt_kernel(*refs, **static):
    _da_prompt_body(pl.program_id(2), pl.num_programs(2) - 1, *refs, **static)


def _da_prompt(qd, ktb, vb, lams, g, lam_init, n_valid, tq, tk, td):
    _, B, F, _ = qd.shape
    rows_last = _round_up(n_valid - (F // tq - 1) * tq, 16)
    lam_spec = pl.BlockSpec((1, DA_DK), lambda b, h, i: (0, 0))
    return pl.pallas_call(
        functools.partial(_da_prompt_kernel, tq=tq, tk=tk, td=td, rows_last=rows_last, lam_init=lam_init),
        grid=(B, DA_HEADS, F // tq),
        in_specs=[
            pl.BlockSpec((2, None, tq, LANES), lambda b, h, i: (0, b, i, h)),
            pl.BlockSpec((None, LANES, F), lambda b, h, i: (b, h, 0)),
            pl.BlockSpec((None, F, LANES), lambda b, h, i: (b, 0, h)),
            lam_spec, lam_spec, lam_spec, lam_spec,
            pl.BlockSpec((1, DA_DV), lambda b, h, i: (0, 0)),
        ],
        out_specs=pl.BlockSpec((None, tq, LANES), lambda b, h, i: (b, i, h)),
        out_shape=jax.ShapeDtypeStruct((B, F, D_MODEL), BF16),
        scratch_shapes=[
            pltpu.VMEM((2, tq, LANES), F32),
            pltpu.VMEM((2, tq, LANES), F32),
            pltpu.VMEM((2, tq, LANES), F32),
        ],
        compiler_params=_params("parallel", "parallel", "parallel"),
        name="da_prompt_attn",
    )(qd, ktb, vb, *lams, g)


def _da_sample_kernel(pt_ref, q_ref, kn_ref, vn_ref, lq1_ref, lk1_ref, lq2_ref, lk2_ref, g_ref, *rest,
                      n_pages, n_new, lam_init):
    del pt_ref
    _da_sample_body(q_ref, kn_ref, vn_ref, lq1_ref, lk1_ref, lq2_ref, lk2_ref, g_ref,
                    rest[:n_pages], rest[n_pages:2 * n_pages], rest[2 * n_pages], n_new=n_new, lam_init=lam_init)


def _da_sample_body(q_ref, kn_ref, vn_ref, lq1_ref, lk1_ref, lq2_ref, lk2_ref, g_ref, kc_refs, vc_refs, o_ref,
                    *, n_new, lam_init):
    page = vc_refs[0].shape[0] // DA_HEADS
    lam = _da_lambda(lq1_ref, lk1_ref, lq2_ref, lk2_ref, lam_init)
    q = q_ref[...]
    row = lax.broadcasted_iota(jnp.int32, (2 * n_new, 1), 0)
    step = jnp.where(row >= n_new, row - n_new, row)
    heads = range(DA_HEADS)
    sls = [slice(h * LANES, (h + 1) * LANES) for h in heads]
    ss = []
    for h in heads:
        kt = jnp.concatenate([r[h].astype(BF16) for r in kc_refs], axis=1)
        ss.append(jnp.dot(q[:, sls[h]], kt, preferred_element_type=F32))
    stats = []
    for h in heads:
        s = ss[h]
        tiles = [s[:, t * LANES:(t + 1) * LANES] for t in range(s.shape[1] // LANES)]
        qf = q[:, sls[h]].astype(F32)
        kn = kn_ref[:, sls[h]]
        s_new = [jnp.where(step >= j, jnp.sum(qf * kn[j:j + 1, :], axis=-1, keepdims=True), MASK_VALUE)
                 for j in range(n_new)]
        m = jnp.max(functools.reduce(jnp.maximum, tiles), axis=-1, keepdims=True)
        m = functools.reduce(jnp.maximum, s_new, m)
        ps = [jnp.exp2(t - m) for t in tiles]
        l = jnp.sum(functools.reduce(jnp.add, ps), axis=-1, keepdims=True)
        stats.append((jnp.concatenate(ps, axis=1).astype(BF16), l, [jnp.exp2(sj - m) for sj in s_new]))
    for h in heads:
        p, l, p_new = stats[h]
        v = jnp.concatenate([r[pl.ds(h, page, stride=DA_HEADS), :].astype(BF16) for r in vc_refs], axis=0)
        acc = jnp.dot(p, v, preferred_element_type=F32)
        for j in range(n_new):
            l = l + p_new[j]
            acc = acc + p_new[j] * vn_ref[j:j + 1, h, :]
        o_ref[:, sls[h]] = _da_finish(acc[:n_new], l[:n_new], acc[n_new:], l[n_new:], lam, g_ref[...], lam_init)


def _da_sample(page_table, layer, q2, kc, vc, kn, vn, lams, g, lam_init):
    DB, n_pages = page_table.shape
    T = kn.shape[1]
    page = kc.shape[-1]
    lam_spec = pl.BlockSpec((1, DA_DK), lambda b, pt: (0, 0))
    k_spec = lambda r: pl.BlockSpec((None, None, DA_HEADS, LANES, page), lambda b, pt: (layer, pt[b, r], 0, 0, 0))
    v_spec = lambda r: pl.BlockSpec((None, None, page * DA_HEADS, DA_DV), lambda b, pt: (layer, pt[b, r], 0, 0))
    grid_spec = pltpu.PrefetchScalarGridSpec(
        num_scalar_prefetch=1,
        grid=(DB,),
        in_specs=[
            pl.BlockSpec((None, 2 * T, D_MODEL), lambda b, pt: (b, 0, 0)),
            pl.BlockSpec((None, T, D_MODEL), lambda b, pt: (b, 0, 0)),
            pl.BlockSpec((None, T, DA_HEADS, DA_DV), lambda b, pt: (b, 0, 0, 0)),
            lam_spec, lam_spec, lam_spec, lam_spec,
            pl.BlockSpec((1, DA_DV), lambda b, pt: (0, 0)),
        ] + [k_spec(r) for r in range(n_pages)] + [v_spec(r) for r in range(n_pages)],
        out_specs=pl.BlockSpec((None, T, D_MODEL), lambda b, pt: (b, 0, 0)),
    )
    return pl.pallas_call(
        functools.partial(_da_sample_kernel, n_pages=n_pages, n_new=T, lam_init=lam_init),
        grid_spec=grid_spec,
        out_shape=jax.ShapeDtypeStruct((DB, T, D_MODEL), F32),
        compiler_params=_params("parallel"),
        name="da_sample_attn",
    )(page_table, q2, kn, vn, *lams, g, *([kc] * n_pages), *([vc] * n_pages))


def _da_fused_kernel(pt_ref, q_ref, kt_ref, v_ref, lq1_ref, lk1_ref, lq2_ref, lk2_ref, g_ref,
                     q2_ref, kn_ref, vn_ref, *rest, n_pages, n_seq, n_new, **prompt_static):
    del pt_ref
    kc_refs, vc_refs = rest[:n_pages], rest[n_pages:2 * n_pages]
    o_ref, os_ref, m_sc, l_sc, acc_sc = rest[2 * n_pages:]
    lams = (lq1_ref, lk1_ref, lq2_ref, lk2_ref)
    i, nq = pl.program_id(2), pl.num_programs(2)
    _da_prompt_body(i, nq - 1, q_ref, kt_ref, v_ref, *lams, g_ref, o_ref, m_sc, l_sc, acc_sc, **prompt_static)
    step = (pl.program_id(0) * pl.num_programs(1) + pl.program_id(1)) * nq + i

    @pl.when(step < n_seq)
    def _():
        _da_sample_body(q2_ref, kn_ref, vn_ref, *lams, g_ref, kc_refs, vc_refs, os_ref,
                        n_new=n_new, lam_init=prompt_static["lam_init"])


def _da_fused(page_table, layer, qd, ktb, vb, q2, kc, vc, kn, vn, lams, g, lam_init, n_valid, tq, tk, td):
    _, B, F, _ = qd.shape
    DB, n_pages = page_table.shape
    T = kn.shape[1]
    page = kc.shape[-1]
    nq = F // tq
    rows_last = _round_up(n_valid - (nq - 1) * tq, 16)
    seq = lambda b, h, i: jnp.minimum((b * DA_HEADS + h) * nq + i, DB - 1)
    lam_spec = pl.BlockSpec((1, DA_DK), lambda b, h, i, pt: (0, 0))
    k_spec = lambda r: pl.BlockSpec((None, None, DA_HEADS, LANES, page),
                                    lambda b, h, i, pt: (layer, pt[seq(b, h, i), r], 0, 0, 0))
    v_spec = lambda r: pl.BlockSpec((None, None, page * DA_HEADS, DA_DV),
                                    lambda b, h, i, pt: (layer, pt[seq(b, h, i), r], 0, 0))
    grid_spec = pltpu.PrefetchScalarGridSpec(
        num_scalar_prefetch=1,
        grid=(B, DA_HEADS, nq),
        in_specs=[
            pl.BlockSpec((2, None, tq, LANES), lambda b, h, i, pt: (0, b, i, h)),
            pl.BlockSpec((None, LANES, F), lambda b, h, i, pt: (b, h, 0)),
            pl.BlockSpec((None, F, LANES), lambda b, h, i, pt: (b, 0, h)),
            lam_spec, lam_spec, lam_spec, lam_spec,
            pl.BlockSpec((1, DA_DV), lambda b, h, i, pt: (0, 0)),
            pl.BlockSpec((None, 2 * T, D_MODEL), lambda b, h, i, pt: (seq(b, h, i), 0, 0)),
            pl.BlockSpec((None, T, D_MODEL), lambda b, h, i, pt: (seq(b, h, i), 0, 0)),
            pl.BlockSpec((None, T, DA_HEADS, DA_DV), lambda b, h, i, pt: (seq(b, h, i), 0, 0, 0)),
        ] + [k_spec(r) for r in range(n_pages)] + [v_spec(r) for r in range(n_pages)],
        out_specs=[
            pl.BlockSpec((None, tq, LANES), lambda b, h, i, pt: (b, i, h)),
            pl.BlockSpec((None, T, D_MODEL), lambda b, h, i, pt: (seq(b, h, i), 0, 0)),
        ],
        scratch_shapes=[
            pltpu.VMEM((2, tq, LANES), F32),
            pltpu.VMEM((2, tq, LANES), F32),
            pltpu.VMEM((2, tq, LANES), F32),
        ],
    )
    return pl.pallas_call(
        functools.partial(_da_fused_kernel, n_pages=n_pages, n_seq=DB, n_new=T, tq=tq, tk=tk, td=td,
                          rows_last=rows_last, lam_init=lam_init),
        grid_spec=grid_spec,
        out_shape=[
            jax.ShapeDtypeStruct((B, F, D_MODEL), BF16),
            jax.ShapeDtypeStruct((DB, T, D_MODEL), F32),
        ],
        compiler_params=_params("arbitrary", "arbitrary", "arbitrary"),
        name="da_attn_fused",
    )(page_table, qd, ktb, vb, *lams, g, q2, kn, vn, *([kc] * n_pages), *([vc] * n_pages))


def _ffn_kernel(x_ref, o_ref, wo_ref, g_ref, win_ref, wout_ref, gout_ref, y_ref, acc_sc, *, final_norm):
    h = x_ref[...] + jnp.dot(o_ref[...].astype(BF16), wo_ref[...], preferred_element_type=F32)
    hn = (_rms(h) * g_ref[...]).astype(BF16)
    acc_sc[...] = h
    for j in range(FFN_HIDDEN // FFN_CHUNK):
        lo = j * FFN_CHUNK
        gate = jnp.dot(hn, win_ref[:, lo:lo + FFN_CHUNK], preferred_element_type=F32)
        up = jnp.dot(hn, win_ref[:, FFN_HIDDEN + lo:FFN_HIDDEN + lo + FFN_CHUNK], preferred_element_type=F32)
        act = (gate * jax.nn.sigmoid(gate) * up).astype(BF16)
        acc_sc[...] += jnp.dot(act, wout_ref[lo:lo + FFN_CHUNK, :], preferred_element_type=F32)
    y = acc_sc[...]
    if final_norm:
        y = _rms(y) * gout_ref[...]
    y_ref[...] = y


def _outproj_ffn(x, o, wo, g, win, wout, gout, tm, final_norm):
    N = x.shape[0]
    ko = o.shape[1]
    return pl.pallas_call(
        functools.partial(_ffn_kernel, final_norm=final_norm),
        grid=(N // tm,),
        in_specs=[
            pl.BlockSpec((tm, D_MODEL), lambda i: (i, 0)),
            pl.BlockSpec((tm, ko), lambda i: (i, 0)),
            _resident((ko, D_MODEL)),
            _resident((1, D_MODEL)),
            _resident((D_MODEL, 2 * FFN_HIDDEN)),
            _resident((FFN_HIDDEN, D_MODEL)),
            _resident((1, D_MODEL)),
        ],
        out_specs=pl.BlockSpec((tm, D_MODEL), lambda i: (i, 0)),
        out_shape=jax.ShapeDtypeStruct((N, D_MODEL), F32),
        scratch_shapes=[pltpu.VMEM((tm, D_MODEL), F32)],
        compiler_params=_params("parallel"),
        name="outproj_ffn",
    )(x, o, wo, g, win, wout, gout)


def _ret_inproj_kernel(x_ref, g_ref, wq_ref, wkt_ref, wv_ref, wg_ref, c_ref, a_ref, b_ref,
                       ct_ref, at_ref, bt_ref, q_ref, kt_ref, v_ref, sg_ref):
    hn = (_rms(x_ref[...]) * g_ref[...]).astype(BF16)
    q = jnp.dot(hn, wq_ref[...], preferred_element_type=F32)
    c, a, b = c_ref[...], a_ref[...], b_ref[...]
    for h in range(RET_HEADS):
        sl = slice(h * RET_DK, (h + 1) * RET_DK)
        qh = q[:, sl]
        qr = qh * c + pltpu.roll(qh, RET_DK - 1, 1) * a + pltpu.roll(qh, 1, 1) * b
        q_ref[:, sl] = qr.astype(BF16)
    kt = lax.dot_general(wkt_ref[...], hn, _NT, preferred_element_type=F32)
    ct, at, bt = ct_ref[...], at_ref[...], bt_ref[...]
    for h in range(RET_HEADS):
        sl = slice(h * RET_DK, (h + 1) * RET_DK)
        kh = kt[sl]
        kr = kh * ct + pltpu.roll(kh, RET_DK - 1, 0) * at + pltpu.roll(kh, 1, 0) * bt
        kt_ref[sl, :] = kr.astype(BF16)
    v_ref[...] = jnp.dot(hn, wv_ref[...], preferred_element_type=F32).astype(BF16)
    gate = jnp.dot(hn, wg_ref[...], preferred_element_type=F32)
    sg_ref[...] = (gate * jax.nn.sigmoid(gate)).astype(BF16)


def _ret_inproj(x, g, wq, wkt, wv, wg, tabs, tm):
    B, F, _ = x.shape
    c, a, b, ct, at, bt = tabs
    tok = lambda bb, i: (bb, i, 0)
    tab = pl.BlockSpec((tm, RET_DK), lambda bb, i: (i, 0))
    tab_t = pl.BlockSpec((RET_DK, tm), lambda bb, i: (0, i))
    return pl.pallas_call(
        _ret_inproj_kernel,
        grid=(B, F // tm),
        in_specs=[
            pl.BlockSpec((None, tm, D_MODEL), tok),
            _resident((1, D_MODEL)),
            _resident((D_MODEL, RET_QK)),
            _resident((RET_QK, D_MODEL)),
            _resident((D_MODEL, RET_VD)),
            _resident((D_MODEL, RET_VD)),
            tab, tab, tab, tab_t, tab_t, tab_t,
        ],
        out_specs=[
            pl.BlockSpec((None, tm, RET_QK), tok),
            pl.BlockSpec((None, RET_QK, tm), lambda bb, i: (bb, 0, i)),
            pl.BlockSpec((None, tm, RET_VD), tok),
            pl.BlockSpec((None, tm, RET_VD), tok),
        ],
        out_shape=[
            jax.ShapeDtypeStruct((B, F, RET_QK), BF16),
            jax.ShapeDtypeStruct((B, RET_QK, F), BF16),
            jax.ShapeDtypeStruct((B, F, RET_VD), BF16),
            jax.ShapeDtypeStruct((B, F, RET_VD), BF16),
        ],
        compiler_params=_params("parallel", "parallel"),
        name="ret_inproj",
    )(x, g, wq, wkt, wv, wg, c, a, b, ct, at, bt)


def _ret_rot_tables(pos):
    angle = 1.0 / (RET_THETA ** jnp.linspace(0.0, 1.0, RET_DK // 2, dtype=F32))
    angle = jnp.repeat(angle, 2)
    ang = pos.astype(F32)[:, None] * angle[None, :]
    cos, sin = jnp.cos(ang), jnp.sin(ang)
    even = (jnp.arange(RET_DK) % 2 == 0)[None, :]
    a = jnp.where(even, -sin, 0.0)
    b = jnp.where(even, 0.0, sin)
    ks = RET_DK ** -0.5
    return cos, a, b, (cos * ks).T, (a * ks).T, (b * ks).T


def _ret_log_decay(h):
    return math.log(1.0 - 2.0 ** (-5.0 - h))


def _ret_gate_out(o, sg):
    return (sg.astype(F32) * _rms(o)).astype(BF16)


def _ret_prompt_kernel(q_ref, kt_ref, v_ref, sg_ref, og_ref, st_ref, s_sc, *, n_valid):
    c_idx = pl.program_id(1)

    @pl.when(c_idx == 0)
    def _():
        s_sc[...] = jnp.zeros(s_sc.shape, F32)

    C = RET_CHUNK
    valid = jnp.clip(n_valid - c_idx * C, 0, C).astype(F32)
    valid_v = jnp.full((1, 1), valid, F32)
    n_row = lax.broadcasted_iota(jnp.int32, (C, 1), 0).astype(F32)
    n_col = lax.broadcasted_iota(jnp.int32, (1, C), 1).astype(F32)
    diff = n_row - n_col
    for h in range(RET_HEADS):
        lg = _ret_log_decay(h)
        q = q_ref[:, h * RET_DK:(h + 1) * RET_DK]
        kt = kt_ref[h * RET_DK:(h + 1) * RET_DK, :]
        v = v_ref[:, h * RET_DV:(h + 1) * RET_DV]
        dmask = jnp.where(diff >= 0, jnp.exp(lg * jnp.maximum(diff, 0.0)), 0.0)
        s = jnp.dot(q, kt, preferred_element_type=F32) * dmask
        inner = jnp.dot(s.astype(BF16), v, preferred_element_type=F32)
        state = s_sc[h]
        q_dec = (q.astype(F32) * jnp.exp(lg * (n_row + 1.0))).astype(BF16)
        cross = jnp.dot(q_dec, state.astype(BF16), preferred_element_type=F32)
        k_decay = jnp.where(n_col < valid, jnp.exp(lg * (valid - 1.0 - n_col)), 0.0)
        k_dec = (kt.astype(F32) * k_decay).astype(BF16)
        s_sc[h] = jnp.exp(lg * valid_v) * state + jnp.dot(k_dec, v, preferred_element_type=F32)
        sl = slice(h * RET_DV, (h + 1) * RET_DV)
        og_ref[:, sl] = _ret_gate_out(inner + cross, sg_ref[:, sl])

    @pl.when(c_idx == pl.num_programs(1) - 1)
    def _():
        st_ref[...] = s_sc[...]


def _ret_prompt(q, kt, v, sg, n_valid):
    B, F, _ = q.shape
    C = RET_CHUNK
    tok = lambda b, c: (b, c, 0)
    return pl.pallas_call(
        functools.partial(_ret_prompt_kernel, n_valid=n_valid),
        grid=(B, F // C),
        in_specs=[
            pl.BlockSpec((None, C, RET_QK), tok),
            pl.BlockSpec((None, RET_QK, C), lambda b, c: (b, 0, c)),
            pl.BlockSpec((None, C, RET_VD), tok),
            pl.BlockSpec((None, C, RET_VD), tok),
        ],
        out_specs=[
            pl.BlockSpec((None, C, RET_VD), tok),
            pl.BlockSpec((None, RET_HEADS, RET_DK, RET_DV), lambda b, c: (b, 0, 0, 0)),
        ],
        out_shape=[
            jax.ShapeDtypeStruct((B, F, RET_VD), BF16),
            jax.ShapeDtypeStruct((B, RET_HEADS, RET_DK, RET_DV), F32),
        ],
        scratch_shapes=[pltpu.VMEM((RET_HEADS, RET_DK, RET_DV), F32)],
        compiler_params=_params("parallel", "arbitrary"),
        name="ret_prompt",
    )(q, kt, v, sg)


def _ret_sample_kernel(q_ref, k_ref, v_ref, sg_ref, st_in_ref, og_ref, st_out_ref, *, n_new):
    T = n_new
    n_row = lax.broadcasted_iota(jnp.int32, (T, 1), 0).astype(F32)
    n_col = lax.broadcasted_iota(jnp.int32, (1, T), 1).astype(F32)
    diff = n_row - n_col
    for h in range(RET_HEADS):
        lg = _ret_log_decay(h)
        q = q_ref[:, h * RET_DK:(h + 1) * RET_DK].astype(F32)
        k = k_ref[:, h * RET_DK:(h + 1) * RET_DK].astype(F32)
        v = v_ref[:, h * RET_DV:(h + 1) * RET_DV].astype(F32)
        dmask = jnp.where(diff >= 0, jnp.exp(lg * jnp.maximum(diff, 0.0)), 0.0)
        s = lax.dot_general(q, k, _NT, preferred_element_type=F32) * dmask
        inner = jnp.dot(s, v, preferred_element_type=F32)
        state = st_in_ref[h]
        q_dec = (q * jnp.exp(lg * (n_row + 1.0))).astype(BF16)
        cross = jnp.dot(q_dec, state.astype(BF16), preferred_element_type=F32)
        k_dec = k * jnp.exp(lg * (T - 1.0 - n_row))
        st_out_ref[h] = math.exp(lg * T) * state + lax.dot_general(k_dec, v, _TN, preferred_element_type=F32)
        sl = slice(h * RET_DV, (h + 1) * RET_DV)
        og_ref[:, sl] = _ret_gate_out(inner + cross, sg_ref[:, sl])


def _ret_sample(q, k, v, sg, state_all, layer, prev):
    DB, T, _ = q.shape
    tok = lambda b: (b, 0, 0)
    st = pl.BlockSpec((None, None, RET_HEADS, RET_DK, RET_DV), lambda b: (layer, b, 0, 0, 0))
    n_in = 5
    in_specs = [
        pl.BlockSpec((None, T, RET_QK), tok),
        pl.BlockSpec((None, T, RET_QK), tok),
        pl.BlockSpec((None, T, RET_VD), tok),
        pl.BlockSpec((None, T, RET_VD), tok),
        st,
    ]
    body = functools.partial(_ret_sample_kernel, n_new=T)
    aliases, extra = {}, ()
    if prev is not None:
        body = _drop_refs(body, n_in, 1)
        in_specs.append(pl.BlockSpec(memory_space=pl.ANY))
        aliases = {n_in: 1}
        extra = (prev,)
    return pl.pallas_call(
        body,
        grid=(DB,),
        in_specs=in_specs,
        out_specs=[pl.BlockSpec((None, T, RET_VD), tok), st],
        out_shape=[
            jax.ShapeDtypeStruct((DB, T, RET_VD), BF16),
            jax.ShapeDtypeStruct(state_all.shape, F32),
        ],
        input_output_aliases=aliases,
        compiler_params=_params("parallel"),
        name="ret_sample",
    )(q, k, v, sg, state_all, *extra)


def kernel(x_prompt, x_sample, cache_k, cache_v, state_ret, page_table, meta_tokens, norm_mix, norm_ffn, norm_out, da_w_in, da_w_out, da_lambda_q1, da_lambda_k1, da_lambda_q2, da_lambda_k2, da_subln, ret_w_in, ret_w_out, ffn_w_in, ffn_w_out):
    B, seq, _ = x_prompt.shape
    DB, T, _ = x_sample.shape
    depth = norm_mix.shape[0]
    L = N_META + seq
    NS = DB * T
    page = cache_k.shape[2]
    past = page_table.shape[1] * page

    tp = _tiles(L)
    ts = {k: min(t, NS) for k, t in _tiles(NS).items()}
    F = _round_up(L, tp["frame"])

    meta = jnp.broadcast_to(meta_tokens.astype(F32)[None], (B, N_META, D_MODEL))
    hp = jnp.concatenate([meta, x_prompt, jnp.zeros((B, F - L, D_MODEL), F32)], axis=1)
    hs = x_sample.reshape(1, NS, D_MODEL)

    pos_p = jnp.arange(F)
    pos_s = past + jnp.arange(NS) % T
    da_tabs_p, da_tabs_s = _da_rope_tables(pos_p), _da_rope_tables(pos_s)
    ret_tabs_p, ret_tabs_s = _ret_rot_tables(pos_p), _ret_rot_tables(pos_s)

    kc = jnp.transpose(cache_k, (0, 1, 3, 4, 5, 2)).reshape(cache_k.shape[0], cache_k.shape[1], DA_HEADS, 2 * DA_DK, page)
    vc = cache_v.reshape(cache_v.shape[0], cache_v.shape[1], page * DA_HEADS, DA_DV)

    row = lambda a: a.astype(F32).reshape(1, -1)
    st_s_all = None
    kv_p = kv_s = None
    st_p_list = []
    for i in range(depth):
        j = i // N_MIXERS
        g_mix = row(norm_mix[i])
        if i % N_MIXERS == 0:
            lam_init = 0.8 - 0.6 * math.exp(-0.3 * i)
            w = da_w_in[j]
            wq = w[:, :D_MODEL].astype(BF16)
            wkt = w[:, D_MODEL:2 * D_MODEL].T.astype(BF16)
            wv = w[:, 2 * D_MODEL:].astype(BF16)
            lams = (row(da_lambda_q1[j]), row(da_lambda_k1[j]), row(da_lambda_q2[j]), row(da_lambda_k2[j]))
            g_sub = row(da_subln[j])
            qd, ktb, vb, kv_p = _da_inproj(hp, g_mix, wq, wkt, wv, da_tabs_p, L, tp["tm_proj"], kv_p)
            qd_s, _, _, kv_s = _da_inproj(hs, g_mix, wq, wkt, wv, da_tabs_s, NS, ts["tm_proj"], kv_s)
            q2 = qd_s.reshape(2, DB, T, D_MODEL).transpose(1, 0, 2, 3).reshape(DB, 2 * T, D_MODEL)
            kn = kv_s[0][-1, 0].T.reshape(DB, T, D_MODEL)
            vn = kv_s[1][-1].reshape(DB, T, DA_HEADS, DA_DV)
            if DB <= B * DA_HEADS * (F // tp["tq"]):
                op, osm = _da_fused(page_table, j, qd, ktb, vb, q2, kc, vc, kn, vn, lams, g_sub, lam_init,
                                    L, tp["tq"], tp["tk"], tp["td"])
            else:
                op = _da_prompt(qd, ktb, vb, lams, g_sub, lam_init, L, tp["tq"], tp["tk"], tp["td"])
                osm = _da_sample(page_table, j, q2, kc, vc, kn, vn, lams, g_sub, lam_init)
            o_p, o_s = op.reshape(B * F, D_MODEL), osm.reshape(NS, D_MODEL)
            wo = da_w_out[j].astype(BF16)
        else:
            w = ret_w_in[j]
            wq = w[:, :RET_QK].astype(BF16)
            wkt = w[:, RET_QK:2 * RET_QK].T.astype(BF16)
            wv = w[:, 2 * RET_QK:2 * RET_QK + RET_VD].astype(BF16)
            wg = w[:, 2 * RET_QK + RET_VD:].astype(BF16)
            q, kt, v, sg = _ret_inproj(hp, g_mix, wq, wkt, wv, wg, ret_tabs_p, tp["tm_proj"])
            q_s, kt_s, v_s, sg_s = _ret_inproj(hs, g_mix, wq, wkt, wv, wg, ret_tabs_s, ts["tm_proj"])
            og, st_p = _ret_prompt(q, kt, v, sg, L)
            per_seq = lambda a: a.reshape(DB, T, a.shape[-1])
            og_s, st_s_all = _ret_sample(per_seq(q_s[0]), per_seq(kt_s[0].T), per_seq(v_s[0]), per_seq(sg_s[0]),
                                         state_ret, j, st_s_all)
            o_p, o_s = og.reshape(B * F, RET_VD), og_s.reshape(NS, RET_VD)
            wo = ret_w_out[j].astype(BF16)
            st_p_list.append(st_p)
        last = i == depth - 1
        ffn = (wo, row(norm_ffn[i]), ffn_w_in[i].astype(BF16), ffn_w_out[i].astype(BF16), row(norm_out))
        hp = _outproj_ffn(hp.reshape(B * F, D_MODEL), o_p, *ffn, tp["tm_ffn"], last).reshape(B, F, D_MODEL)
        hs = _outproj_ffn(hs.reshape(NS, D_MODEL), o_s, *ffn, ts["tm_ffn"], last).reshape(1, NS, D_MODEL)
    y_prompt = hp[:, N_META:L]
    y_sample = hs.reshape(DB, T, D_MODEL)
    n_att = kv_p[0].shape[0]
    new_k_prompt = kv_p[0].reshape(n_att, B, DA_HEADS, 2, DA_DK, L).transpose(0, 1, 5, 2, 3, 4)
    new_k_sample = kv_s[0].reshape(n_att, DA_HEADS, 2, DA_DK, DB, T).transpose(0, 4, 5, 1, 2, 3)
    new_v_sample = kv_s[1].reshape(n_att, DB, T, DA_HEADS, DA_DV)
    return (y_prompt, y_sample, new_k_prompt, kv_p[1], new_k_sample, new_v_sample,
            jnp.stack(st_p_list), st_s_all)
```
